```python
import jax, jax.numpy as jnp
from jax import lax
import numpy as np

D_MODEL = 1024
BATCH = 8
SEQ = 8192
DEPTH = 1
DEC_BATCH = 128
DEC_SEQ = 8
PAST_LEN = 8192
PAGE_SIZE = 128

ATT_WINDOWS = (128, 512, 2048)
ATT_DILATIONS = (1, 4, 16)
N_GROUPS = 3
HEADS_PER_GROUP = 8
HEAD_DIM = 64
N_ATT_HEADS = N_GROUPS * HEADS_PER_GROUP
ATT_QKV = N_ATT_HEADS * HEAD_DIM
ATT_OUT = HEADS_PER_GROUP * HEAD_DIM
ATT_BLOCK = 128
LRU_WIDTH = D_MODEL
LRU_BLOCKS = 16
LRU_BLOCK_W = LRU_WIDTH // LRU_BLOCKS
CONV_WIDTH = 4
LRU_C = 8.0
N_EXPERTS = 32
TOP_K = 4
D_FF = D_MODEL
SWIGLU_ALPHA = 1.702
SWIGLU_LIMIT = 7.0
MOE_BLOCK = 128
RMS_EPS = 1e-6
IN_COLS = 3 * ATT_QKV + 2 * LRU_WIDTH + 2 * D_MODEL
SPLITS = (ATT_QKV, 2 * ATT_QKV, 3 * ATT_QKV, 3 * ATT_QKV + LRU_WIDTH,
          3 * ATT_QKV + 2 * LRU_WIDTH, 3 * ATT_QKV + 2 * LRU_WIDTH + D_MODEL)

kernel_name = 'hybrid_dilated_attn_rglru_moe_step'

F32 = jnp.float32


def _rmsnorm(x, g):
    xf = x.astype(F32)
    return (xf * lax.rsqrt(jnp.mean(xf * xf, axis=-1, keepdims=True) + RMS_EPS) * g.astype(F32)).astype(x.dtype)


def _alibi_slopes(n):
    return jnp.asarray(np.array([2.0 ** (-8.0 * (i + 1) / n) for i in range(n)], np.float32))


def _dilated_group_prompt(q, k, v, slopes, window, dilation):
    b, s, h, e = q.shape
    n = s // dilation
    n_back = window // dilation
    nb = -(-n // ATT_BLOCK)
    n_pad = nb * ATT_BLOCK

    def by_residue(t, front):
        t = t.reshape(b, n, dilation, h, e).transpose(0, 2, 1, 3, 4)
        return jnp.pad(t, ((0, 0), (0, 0), (front, n_pad - n), (0, 0), (0, 0)))

    qb = by_residue(q * HEAD_DIM ** -0.5, 0).reshape(b, dilation, nb, ATT_BLOCK, h, e)

    def band(t):
        tp = by_residue(t, ATT_BLOCK).reshape(b, dilation, nb + 1, ATT_BLOCK, h, e)
        return jnp.concatenate([tp[:, :, :-1], tp[:, :, 1:]], axis=3)

    kb, vb = band(k), band(v)
    step = jnp.arange(ATT_BLOCK)[:, None] + ATT_BLOCK - jnp.arange(2 * ATT_BLOCK)[None, :]
    key_row = (jnp.arange(nb)[:, None, None] - 1) * ATT_BLOCK + jnp.arange(2 * ATT_BLOCK)[None, None, :]
    mask = (step >= 0) & (step <= n_back) & (key_row >= 0)
    bias = -slopes.astype(F32)[:, None, None] * (step * dilation).astype(F32)
    sc = jnp.einsum('bgnqhe,bgnkhe->bgnhqk', qb, kb).astype(F32) + bias
    sc = jnp.where(mask[:, None], sc, -jnp.inf)
    m = jnp.max(sc, axis=-1, keepdims=True)
    pexp = jnp.exp(sc - m)
    den = jnp.sum(pexp, axis=-1)
    o = jnp.einsum('bgnhqk,bgnkhe->bgnqhe', pexp, vb.astype(F32)) / jnp.swapaxes(den, 3, 4)[..., None]
    lse = jnp.swapaxes(m[..., 0] + jnp.log(den), 3, 4)
    o = o.reshape(b, dilation, n_pad, h, e)[:, :, :n].transpose(0, 2, 1, 3, 4).reshape(b, s, h, e)
    lse = lse.reshape(b, dilation, n_pad, h)[:, :, :n].transpose(0, 2, 1, 3).reshape(b, s, h)
    return o, lse


def _dilated_group_sample(q, k_all, v_all, slopes, window, dilation, n_past):
    t = q.shape[1]
    steps = jnp.arange(window // dilation + 1)
    idx = n_past + jnp.arange(t)[:, None] - dilation * steps[None, :]
    valid = idx >= 0
    idx = jnp.maximum(idx, 0)
    kg = k_all[:, idx]
    vg = v_all[:, idx]
    sc = jnp.einsum('bthe,btmhe->bthm', q * HEAD_DIM ** -0.5, kg).astype(F32)
    sc = sc - slopes.astype(F32)[:, None] * (dilation * steps).astype(F32)[None, :]
    sc = jnp.where(valid[:, None, :], sc, -jnp.inf)
    m = jnp.max(sc, axis=-1, keepdims=True)
    pexp = jnp.exp(sc - m)
    den = jnp.sum(pexp, axis=-1)
    o = jnp.einsum('bthm,btmhe->bthe', pexp, vg.astype(F32)) / den[..., None]
    return o, m[..., 0] + jnp.log(den)


def _merge_groups(outs, lses):
    w = jax.nn.softmax(jnp.stack(lses, axis=0), axis=0)
    return jnp.sum(w[..., None] * jnp.stack(outs, axis=0), axis=0)


def _attn_prompt(q, k, v, slopes):
    outs, lses, new_kv = [], [], []
    for g in range(N_GROUPS):
        hs = slice(g * HEADS_PER_GROUP, (g + 1) * HEADS_PER_GROUP)
        o, l = _dilated_group_prompt(q[:, :, hs], k[:, :, hs], v[:, :, hs], slopes[hs],
                                     ATT_WINDOWS[g], ATT_DILATIONS[g])
        outs.append(o)
        lses.append(l)
        keep = min(ATT_WINDOWS[g], q.shape[1])
        new_kv.append(jnp.stack([k[:, -keep:, hs], v[:, -keep:, hs]], axis=2))
    return _merge_groups(outs, lses), new_kv


def _attn_sample(q, k, v, slopes, kv_bufs):
    outs, lses, new_kv = [], [], []
    for g in range(N_GROUPS):
        hs = slice(g * HEADS_PER_GROUP, (g + 1) * HEADS_PER_GROUP)
        kv_new = jnp.stack([k[:, :, hs], v[:, :, hs]], axis=2)
        kv_all = jnp.concatenate([kv_bufs[g].astype(kv_new.dtype), kv_new], axis=1)
        o, l = _dilated_group_sample(q[:, :, hs], kv_all[:, :, 0], kv_all[:, :, 1], slopes[hs],
                                     ATT_WINDOWS[g], ATT_DILATIONS[g], kv_bufs[g].shape[1])
        outs.append(o)
        lses.append(l)
        keep = min(ATT_WINDOWS[g], kv_all.shape[1])
        new_kv.append(kv_all[:, -keep:])
    return _merge_groups(outs, lses), new_kv


def _rglru_branch(xb, conv_buf, h0, p, reset_first):
    b, t, w = xb.shape
    xfull = jnp.concatenate([conv_buf.astype(xb.dtype), xb], axis=1)
    xc = p['conv_b'] + sum(xfull[:, j:j + t] * p['conv_w'][j] for j in range(CONV_WIDTH))
    xg = xc.reshape(b, t, LRU_BLOCKS, LRU_BLOCK_W)
    r = jax.nn.sigmoid(jnp.einsum('btnc,ncd->btnd', xg, p['lru_wa']) + p['lru_ba']).reshape(b, t, w)
    i = jax.nn.sigmoid(jnp.einsum('btnc,ncd->btnd', xg, p['lru_wi']) + p['lru_bi']).reshape(b, t, w)
    log_a = -LRU_C * r.astype(F32) * jax.nn.softplus(-p['lru_lambda'].astype(F32))
    a = jnp.exp(log_a)
    mult = jnp.sqrt(-jnp.expm1(2.0 * log_a))
    if reset_first:
        mult = mult.at[:, 0].set(1.0)
    u = mult * (i * xc).astype(F32)

    def step(h, au):
        a_t, u_t = au
        h = a_t * h + u_t
        return h, h

    h_last, hs = lax.scan(step, h0.astype(F32), (jnp.swapaxes(a, 0, 1), jnp.swapaxes(u, 0, 1)))
    return jnp.swapaxes(hs, 0, 1), xfull[:, -(CONV_WIDTH - 1):], h_last


def _moe(x, p):
    b, t, d = x.shape
    n_tok = b * t
    xf = x.reshape(n_tok, d)
    logits = (xf @ p['w_router'] + p['b_router']).astype(F32)
    top_val, top_idx = lax.top_k(logits, TOP_K)
    gates = jax.nn.softmax(top_val, axis=-1)
    n_assign = n_tok * TOP_K
    e_flat = top_idx.reshape(-1).astype(jnp.int32)
    tok_flat = jnp.arange(n_assign, dtype=jnp.int32) // TOP_K
    order = jnp.argsort(e_flat)
    e_sorted = e_flat[order]
    counts = jnp.bincount(e_flat, length=N_EXPERTS)
    padded = (counts + MOE_BLOCK - 1) // MOE_BLOCK * MOE_BLOCK
    starts = jnp.cumsum(counts) - counts
    pends = jnp.cumsum(padded)
    pstarts = pends - padded
    dest = pstarts[e_sorted] + jnp.arange(n_assign, dtype=jnp.int32) - starts[e_sorted]
    n_blocks = -(-(n_assign + N_EXPERTS * (MOE_BLOCK - 1)) // MOE_BLOCK)
    n_rows = n_blocks * MOE_BLOCK
    row_tok = jnp.full((n_rows,), n_tok, jnp.int32).at[dest].set(tok_flat[order])
    row_gate = jnp.zeros((n_rows,), F32).at[dest].set(gates.reshape(-1)[order])
    block_expert = jnp.minimum(jnp.searchsorted(pends, jnp.arange(n_blocks) * MOE_BLOCK, side='right'),
                               N_EXPERTS - 1)
    x_pad = jnp.concatenate([xf, jnp.zeros((1, d), xf.dtype)], axis=0)
    x_rows = x_pad[row_tok].reshape(n_blocks, MOE_BLOCK, d)
    w1, b1, w2, b2 = p['w1'], p['b1'], p['w2'], p['b2']

    def expert_block(args):
        xb, e = args
        hmid = (xb @ w1[e] + b1[e]).astype(F32)
        x_glu = jnp.minimum(hmid[:, ::2], SWIGLU_LIMIT)
        x_lin = jnp.clip(hmid[:, 1::2], -SWIGLU_LIMIT, SWIGLU_LIMIT)
        act = x_glu * jax.nn.sigmoid(SWIGLU_ALPHA * x_glu) * (x_lin + 1.0)
        return (act.astype(xb.dtype) @ w2[e] + b2[e]).astype(F32)

    y_rows = lax.map(expert_block, (x_rows, block_expert)).reshape(n_rows, d)
    y = jax.ops.segment_sum(y_rows * row_gate[:, None], row_tok, num_segments=n_tok + 1)[:n_tok]
    return y.reshape(b, t, d).astype(x.dtype)


def _layer(x, p, slopes, kv_bufs, conv_buf, h0):
    prompt = kv_bufs is None
    b, t, _ = x.shape
    xn = _rmsnorm(x, p['norm_mix'])
    z = xn @ p['w_in']
    q, k, v, xb, yg, ga, gb = jnp.split(z, SPLITS, axis=-1)
    q = q.reshape(b, t, N_ATT_HEADS, HEAD_DIM)
    k = k.reshape(b, t, N_ATT_HEADS, HEAD_DIM)
    v = v.reshape(b, t, N_ATT_HEADS, HEAD_DIM)
    if prompt:
        att, new_kv = _attn_prompt(q, k, v, slopes)
        conv_buf = jnp.zeros((b, CONV_WIDTH - 1, LRU_WIDTH), xb.dtype)
        h0 = jnp.zeros((b, LRU_WIDTH), F32)
    else:
        att, new_kv = _attn_sample(q, k, v, slopes, kv_bufs)
    h, new_conv, h_last = _rglru_branch(xb, conv_buf, h0, p, prompt)
    y_b = jax.nn.gelu(yg.astype(F32)) * h
    y_a = att.reshape(b, t, ATT_OUT) @ p['w_pa']
    merged = jax.nn.sigmoid(ga.astype(F32)) * y_a + jax.nn.sigmoid(gb.astype(F32)) * (y_b @ p['w_pb'])
    x = x + (merged @ p['w_o']).astype(x.dtype)
    x = x + _moe(_rmsnorm(x, p['norm_ffn']), p)
    return x, new_kv, new_conv, h_last


def setup_inputs(seed: int = 0) -> dict:
    key = jax.random.key(seed)
    ks = iter(jax.random.split(key, 40))

    def nrm(shape, scale):
        return jax.random.normal(next(ks), shape, F32) * scale

    L = DEPTH
    clen = [min(w, PAST_LEN) for w in ATT_WINDOWS]
    u = jax.random.uniform(next(ks), (L, LRU_WIDTH), F32, 0.9, 0.999)
    s = u ** (1.0 / LRU_C)
    lam = jnp.log(s) - jnp.log1p(-s)
    return {
        'x_prompt': nrm((BATCH, SEQ, D_MODEL), 1.0),
        'x_sample': nrm((DEC_BATCH, DEC_SEQ, D_MODEL), 1.0),
        'cache_kv_g1': nrm((L, DEC_BATCH, clen[0], 2, HEADS_PER_GROUP, HEAD_DIM), 1.0),
        'cache_kv_g2': nrm((L, DEC_BATCH, clen[1], 2, HEADS_PER_GROUP, HEAD_DIM), 1.0),
        'cache_kv_g3': nrm((L, DEC_BATCH, clen[2], 2, HEADS_PER_GROUP, HEAD_DIM), 1.0),
        'state_conv': nrm((L, DEC_BATCH, CONV_WIDTH - 1, LRU_WIDTH), 1.0),
        'state_h': nrm((L, DEC_BATCH, LRU_WIDTH), 0.5),
        'norm_mix': 1.0 + nrm((L, D_MODEL), 0.05),
        'w_in': nrm((L, D_MODEL, IN_COLS), D_MODEL ** -0.5),
        'w_pa': nrm((L, ATT_OUT, D_MODEL), ATT_OUT ** -0.5),
        'w_pb': nrm((L, LRU_WIDTH, D_MODEL), LRU_WIDTH ** -0.5),
        'w_o': nrm((L, D_MODEL, D_MODEL), D_MODEL ** -0.5),
        'conv_w': nrm((L, CONV_WIDTH, LRU_WIDTH), CONV_WIDTH ** -0.5),
        'conv_b': nrm((L, LRU_WIDTH), 0.02),
        'lru_wa': nrm((L, LRU_BLOCKS, LRU_BLOCK_W, LRU_BLOCK_W), LRU_BLOCK_W ** -0.5),
        'lru_ba': nrm((L, LRU_BLOCKS, LRU_BLOCK_W), 0.02),
        'lru_wi': nrm((L, LRU_BLOCKS, LRU_BLOCK_W, LRU_BLOCK_W), LRU_BLOCK_W ** -0.5),
        'lru_bi': nrm((L, LRU_BLOCKS, LRU_BLOCK_W), 0.02),
        'lru_lambda': lam,
        'norm_ffn': 1.0 + nrm((L, D_MODEL), 0.05),
        'w_router': nrm((L, D_MODEL, N_EXPERTS), D_MODEL ** -0.5),
        'b_router': nrm((L, N_EXPERTS), 0.01),
        'w1': nrm((L, N_EXPERTS, D_MODEL, 2 * D_FF), D_MODEL ** -0.5),
        'b1': nrm((L, N_EXPERTS, 2 * D_FF), 0.02),
        'w2': nrm((L, N_EXPERTS, D_FF, D_MODEL), D_FF ** -0.5),
        'b2': nrm((L, N_EXPERTS, D_MODEL), 0.02),
        'norm_final': 1.0 + nrm((D_MODEL,), 0.05),
    }


def reference(x_prompt, x_sample, cache_kv_g1, cache_kv_g2, cache_kv_g3, state_conv, state_h,
              norm_mix, w_in, w_pa, w_pb, w_o, conv_w, conv_b, lru_wa, lru_ba, lru_wi, lru_bi,
              lru_lambda, norm_ffn, w_router, b_router, w1, b1, w2, b2, norm_final):
    slopes = _alibi_slopes(N_ATT_HEADS)
    xp, xs = x_prompt, x_sample
    kvp = [[] for _ in range(N_GROUPS)]
    kvs = [[] for _ in range(N_GROUPS)]
    convp, hp, convs, hsm = [], [], [], []
    for l in range(DEPTH):
        p = dict(norm_mix=norm_mix[l], w_in=w_in[l], w_pa=w_pa[l], w_pb=w_pb[l], w_o=w_o[l],
                 conv_w=conv_w[l], conv_b=conv_b[l], lru_wa=lru_wa[l], lru_ba=lru_ba[l],
                 lru_wi=lru_wi[l], lru_bi=lru_bi[l], lru_lambda=lru_lambda[l], norm_ffn=norm_ffn[l],
                 w_router=w_router[l], b_router=b_router[l], w1=w1[l], b1=b1[l], w2=w2[l], b2=b2[l])
        xp, nkv_p, nconv_p, nh_p = _layer(xp, p, slopes, None, None, None)
        xs, nkv_s, nconv_s, nh_s = _layer(xs, p, slopes, [cache_kv_g1[l], cache_kv_g2[l], cache_kv_g3[l]],
                                          state_conv[l], state_h[l])
        for g in range(N_GROUPS):
            kvp[g].append(nkv_p[g])
            kvs[g].append(nkv_s[g])
        convp.append(nconv_p)
        hp.append(nh_p)
        convs.append(nconv_s)
        hsm.append(nh_s)
    y_prompt = _rmsnorm(xp, norm_final)
    y_sample = _rmsnorm(xs, norm_final)
    kv_g1_prompt, kv_g2_prompt, kv_g3_prompt = [jnp.stack(a, axis=0) for a in kvp]
    kv_g1_sample, kv_g2_sample, kv_g3_sample = [jnp.stack(a, axis=0) for a in kvs]
    conv_prompt = jnp.stack(convp, axis=0)
    h_prompt = jnp.stack(hp, axis=0)
    conv_sample = jnp.stack(convs, axis=0)
    h_sample = jnp.stack(hsm, axis=0)
    return (y_prompt, y_sample, kv_g1_prompt, kv_g2_prompt, kv_g3_prompt, conv_prompt, h_prompt,
            kv_g1_sample, kv_g2_sample, kv_g3_sample, conv_sample, h_sample)
```

```python
import functools

import numpy as np
import jax
import jax.numpy as jnp
from jax import lax
from jax.experimental import pallas as pl
from jax.experimental.pallas import tpu as pltpu

F32 = jnp.float32
BF16 = jnp.bfloat16

N_GROUPS = 3
HEADS_PER_GROUP = 8
HEAD_DIM = 64
GROUP_W = HEADS_PER_GROUP * HEAD_DIM
ATT_WINDOWS = (128, 512, 2048)
ATT_DILATIONS = (1, 4, 16)
ATT_BLOCK = 128
N_ATT_HEADS = N_GROUPS * HEADS_PER_GROUP
CONV_WIDTH = 4
LRU_C = 8.0
LRU_BLOCK_W = 64
N_EXPERTS = 32
TOP_K = 4
SWIGLU_ALPHA = 1.702
SWIGLU_LIMIT = 7.0
RMS_EPS = 1e-6
NEG_INF = float("-inf")

LANES = 128
SUBLANES = 8
MXU_DIM = 256
VMEM_LIMIT_BYTES = 56 * 1024 * 1024

COL_Q, COL_K, COL_V = 8, 11, 14
N_COL_BLOCKS = 17
MOE_BLOCK = 256

_SLOPES = [float(np.float32(2.0 ** (-8.0 * (i + 1) / N_ATT_HEADS))) for i in range(N_ATT_HEADS)]


def _params(*sem):
    return pltpu.CompilerParams(dimension_semantics=sem, vmem_limit_bytes=VMEM_LIMIT_BYTES)


def _sigmoid(x):
    return 1.0 / (1.0 + jnp.exp(-x))


def _log2(n):
    assert n > 0 and n & (n - 1) == 0, n
    return n.bit_length() - 1


def _in_proj_kernel(x_ref, g_ref, w_ref, o_ref, xn_ref):
    @pl.when(pl.program_id(1) == 0)
    def _():
        x = x_ref[...]
        ms = jnp.mean(x * x, axis=-1, keepdims=True)
        xn_ref[...] = (x * lax.rsqrt(ms + RMS_EPS) * g_ref[...]).astype(BF16)

    o_ref[...] = jnp.dot(xn_ref[...], w_ref[...], preferred_element_type=F32).astype(o_ref.dtype)


def _in_proj(x2d, gain, w, out_dtype):
    n, d = x2d.shape
    cols = w.shape[1]
    tm = min(512, n)
    tn = cols // 4
    return pl.pallas_call(
        _in_proj_kernel,
        grid=(n // tm, cols // tn),
        in_specs=[
            pl.BlockSpec((tm, d), lambda i, j: (i, 0)),
            pl.BlockSpec((1, d), lambda i, j: (0, 0)),
            pl.BlockSpec((d, tn), lambda i, j: (0, j)),
        ],
        out_specs=pl.BlockSpec((tm, tn), lambda i, j: (i, j)),
        out_shape=jax.ShapeDtypeStruct((n, cols), out_dtype),
        scratch_shapes=[pltpu.VMEM((tm, d), BF16)],
        compiler_params=_params("parallel", "arbitrary"),
        name="in_proj",
    )(x2d, gain, w)


def _attn_prompt_kernel(q_ref, kp_ref, kc_ref, vp_ref, vc_ref, o_ref, l_ref, *, slopes, dilation):
    blk = ATT_BLOCK
    j = pl.program_id(2)
    q = q_ref[...] * (HEAD_DIM ** -0.5)
    k = jnp.concatenate([kp_ref[...], kc_ref[...]], axis=0)
    v = jnp.concatenate([vp_ref[...], vc_ref[...]], axis=0)
    qq = lax.broadcasted_iota(jnp.int32, (blk, 2 * blk), 0)
    kk = lax.broadcasted_iota(jnp.int32, (blk, 2 * blk), 1)
    step = qq + blk - kk
    valid = (step >= 0) & (step <= blk) & ((kk >= blk) | (j > 0))
    dist = (step * dilation).astype(F32)
    low = lax.broadcasted_iota(jnp.int32, (blk, LANES), 1) < HEAD_DIM
    for p in range(HEADS_PER_GROUP // 2):
        cs = slice(p * LANES, (p + 1) * LANES)
        qg, kg, vg = q[:, cs], k[:, cs], v[:, cs]
        outs, lses = [], []
        for hh in range(2):
            sel = low if hh == 0 else jnp.logical_not(low)
            qm = jnp.where(sel, qg, jnp.zeros_like(qg))
            s = lax.dot_general(qm, kg, (((1,), (1,)), ((), ())), preferred_element_type=F32)
            s = s - slopes[2 * p + hh] * dist
            s = jnp.where(valid, s, NEG_INF)
            m = jnp.max(s, axis=-1, keepdims=True)
            e = jnp.exp(s - m)
            den = jnp.sum(e, axis=-1, keepdims=True)
            o = jnp.dot(e.astype(BF16), vg, preferred_element_type=F32) / den
            outs.append(o)
            lses.append(m + jnp.log(den))
        o_ref[:, cs] = jnp.where(low, outs[0], outs[1])
        l_ref[:, cs] = jnp.where(low, lses[0], lses[1])


def _attn_prompt(z, batch, seq, g):
    d = ATT_DILATIONS[g]
    n_tok, cols = z.shape
    ncb = cols // GROUP_W
    nb = seq // (d * ATT_BLOCK)
    z2 = z.reshape(n_tok // d, d * cols)
    blk = (ATT_BLOCK, GROUP_W)

    def cur(c):
        return pl.BlockSpec(blk, lambda b, r, j: (b * nb + j, r * ncb + c))

    def prev(c):
        return pl.BlockSpec(blk, lambda b, r, j: (b * nb + jnp.maximum(j - 1, 0), r * ncb + c))

    out_spec = pl.BlockSpec(blk, lambda b, r, j: (b * nb + j, r))
    out_sds = jax.ShapeDtypeStruct((n_tok // d, d * GROUP_W), F32)
    kern = functools.partial(_attn_prompt_kernel, slopes=_SLOPES[g * HEADS_PER_GROUP:(g + 1) * HEADS_PER_GROUP],
                             dilation=d)
    o, l = pl.pallas_call(
        kern,
        grid=(batch, d, nb),
        in_specs=[cur(COL_Q + g), prev(COL_K + g), cur(COL_K + g), prev(COL_V + g), cur(COL_V + g)],
        out_specs=[out_spec, out_spec],
        out_shape=[out_sds, out_sds],
        compiler_params=_params("parallel", "parallel", "arbitrary"),
        name=f"attn_prompt_g{g}",
    )(z2, z2, z2, z2, z2)
    return o.reshape(n_tok, GROUP_W), l.reshape(n_tok, GROUP_W)


def _attn_sample_kernel(q_ref, k_ref, v_ref, c_ref, sl_ref, o_ref, l_ref, cout_ref, *, window, dilation):
    t_new = q_ref.shape[0]
    rows = HEADS_PER_GROUP * t_new
    q = q_ref[...] * (HEAD_DIM ** -0.5)
    kn, vn = k_ref[...], v_ref[...]
    cout_ref[0:window - t_new, :] = c_ref[t_new:window, :]
    cout_ref[window - t_new:window, 0:GROUP_W] = kn
    cout_ref[window - t_new:window, GROUP_W:2 * GROUP_W] = vn
    pad = jnp.zeros((LANES - t_new, GROUP_W), F32)
    k_all = jnp.concatenate([c_ref[:, 0:GROUP_W], kn, pad], axis=0).astype(BF16)
    v_all = jnp.concatenate([c_ref[:, GROUP_W:2 * GROUP_W], vn, pad], axis=0).astype(BF16)
    n_keys = window + LANES
    qt = jnp.concatenate([q] * HEADS_PER_GROUP, axis=0)
    row_h = lax.broadcasted_iota(jnp.int32, (rows, GROUP_W), 0) >> _log2(t_new)
    col_h = lax.broadcasted_iota(jnp.int32, (rows, GROUP_W), 1) >> _log2(HEAD_DIM)
    qbd = jnp.where(row_h == col_h, qt, 0.0).astype(BF16)
    s = lax.dot_general(qbd, k_all, (((1,), (1,)), ((), ())), preferred_element_type=F32)
    r = lax.broadcasted_iota(jnp.int32, (rows, n_keys), 0)
    c = lax.broadcasted_iota(jnp.int32, (rows, n_keys), 1)
    t = r & (t_new - 1)
    dist = jnp.where(c < window, window + t - c, t - (c - window))
    _log2(dilation)
    valid = (dist >= 0) & (dist <= window) & ((dist & (dilation - 1)) == 0) & (c < window + t_new)
    s = s - sl_ref[:, 0:1] * dist.astype(F32)
    s = jnp.where(valid, s, NEG_INF)
    m = jnp.max(s, axis=-1, keepdims=True)
    e = jnp.exp(s - m)
    den = jnp.sum(e, axis=-1, keepdims=True)
    o_full = jnp.dot(e.astype(BF16), v_all, preferred_element_type=F32) / den
    lse = m + jnp.log(den)
    out_h = lax.broadcasted_iota(jnp.int32, (t_new, GROUP_W), 1) >> _log2(HEAD_DIM)
    o = jnp.zeros((t_new, GROUP_W), F32)
    l = jnp.zeros((t_new, GROUP_W), F32)
    for h in range(HEADS_PER_GROUP):
        rs = slice(h * t_new, (h + 1) * t_new)
        o = jnp.where(out_h == h, o_full[rs, :], o)
        l = jnp.where(out_h == h, lse[rs, :], l)
    o_ref[...] = o
    l_ref[...] = l


def _attn_sample(z, cache, g):
    batch, window = cache.shape[0], cache.shape[1]
    n_tok = z.shape[0]
    t_new = n_tok // batch
    d = ATT_DILATIONS[g]
    c2 = cache.reshape(batch, window, 2 * GROUP_W)
    slopes = np.repeat(np.asarray(_SLOPES[g * HEADS_PER_GROUP:(g + 1) * HEADS_PER_GROUP], np.float32), t_new)
    slopes = jnp.asarray(np.broadcast_to(slopes[:, None], (HEADS_PER_GROUP * t_new, LANES)))

    def col(c):
        return pl.BlockSpec((t_new, GROUP_W), lambda b: (b, c))

    tok_sds = jax.ShapeDtypeStruct((n_tok, GROUP_W), F32)
    o, l, cout = pl.pallas_call(
        functools.partial(_attn_sample_kernel, window=window, dilation=d),
        grid=(batch,),
        in_specs=[col(COL_Q + g), col(COL_K + g), col(COL_V + g),
                  pl.BlockSpec((None, window, 2 * GROUP_W), lambda b: (b, 0, 0)),
                  pl.BlockSpec(slopes.shape, lambda b: (0, 0))],
        out_specs=[pl.BlockSpec((t_new, GROUP_W), lambda b: (b, 0)),
                   pl.BlockSpec((t_new, GROUP_W), lambda b: (b, 0)),
                   pl.BlockSpec((None, window, 2 * GROUP_W), lambda b: (b, 0, 0))],
        out_shape=[tok_sds, tok_sds, jax.ShapeDtypeStruct(c2.shape, F32)],
        compiler_params=_params("parallel"),
        name=f"attn_sample_g{g}",
    )(z, z, z, c2, slopes)
    return o, l, cout.reshape(cache.shape)


def _gelu_tanh(x):
    return x * (0.5 * (1.0 + jnp.tanh(np.sqrt(2.0 / np.pi).astype(np.float32) * (x + 0.044715 * (x * x * x)))))


def _rglru_kernel(xb_ref, yg_ref, c0_ref, h0_ref, cw_ref, cb_ref, wa_ref, wi_ref, ba_ref, bi_ref, lam_ref,
                  yb_ref, cout_ref, hout_ref, xpad, a_s, u_s, h_s, *, tc, seg, reset_first):
    nb = xb_ref.shape[0]
    ngl = a_s.shape[0]
    j = pl.program_id(1)

    @pl.when(j == 0)
    def _():
        xpad[:, 0:SUBLANES, :] = c0_ref[...]
        h_s[...] = h0_ref[...]

    @pl.when(j > 0)
    def _():
        xpad[:, 0:SUBLANES, :] = xpad[:, tc:tc + SUBLANES, :]

    xpad[:, SUBLANES:SUBLANES + tc, :] = xb_ref[...].astype(F32)
    cout_ref[...] = xpad[:, tc:tc + SUBLANES, :]

    lam = lam_ref[...]
    softplus_neg = jnp.maximum(-lam, 0.0) + jnp.log1p(jnp.exp(-jnp.abs(lam)))
    first = (lax.broadcasted_iota(jnp.int32, (tc, 1), 0) == 0) & (j == 0)
    gw = wa_ref.shape[1]
    for b in range(nb):
        xc = cb_ref[...]
        for tap in range(CONV_WIDTH):
            off = SUBLANES - (CONV_WIDTH - 1) + tap
            xc = xc + xpad[b, off:off + tc, :] * cw_ref[tap:tap + 1, :]
        xcb = xc.astype(BF16)
        ra, ri = [], []
        for blk in range(wa_ref.shape[0]):
            xs = xcb[:, blk * gw:(blk + 1) * gw]
            ra.append(jnp.dot(xs, wa_ref[blk], preferred_element_type=F32))
            ri.append(jnp.dot(xs, wi_ref[blk], preferred_element_type=F32))
        r = _sigmoid(jnp.concatenate(ra, axis=-1) + ba_ref[...])
        i = _sigmoid(jnp.concatenate(ri, axis=-1) + bi_ref[...])
        log_a = -LRU_C * r * softplus_neg
        a = jnp.exp(log_a)
        mult = jnp.sqrt(1.0 - jnp.exp(2.0 * log_a))
        if reset_first:
            mult = jnp.where(first, 1.0, mult)
        u = mult * (i * xc)
        for c in range(ngl):
            a_s[c, b * seg:b * seg + tc, :] = a[:, c * LANES:(c + 1) * LANES]
            u_s[c, b * seg:b * seg + tc, :] = u[:, c * LANES:(c + 1) * LANES]

    def step(t, hs):
        idx = pl.ds(t, nb, stride=seg)
        out = []
        for c in range(ngl):
            h = a_s[c, idx, :] * hs[c] + u_s[c, idx, :]
            u_s[c, idx, :] = h
            out.append(h)
        return tuple(out)

    hs = lax.fori_loop(0, tc, step, tuple(h_s[:, c * LANES:(c + 1) * LANES] for c in range(ngl)))
    h = jnp.concatenate(hs, axis=-1)
    h_s[...] = h
    hout_ref[...] = h
    for b in range(nb):
        hb = jnp.concatenate([u_s[c, b * seg:b * seg + tc, :] for c in range(ngl)], axis=-1)
        yb_ref[b] = (_gelu_tanh(yg_ref[b].astype(F32)) * hb).astype(yb_ref.dtype)


def _rglru(z3, conv0, h0, p, reset_first):
    batch, t, _ = z3.shape
    width = h0.shape[-1]
    nb = SUBLANES
    tc = min(128, t)
    seg = tc + SUBLANES
    c0 = jnp.pad(conv0, ((0, 0), (SUBLANES - (CONV_WIDTH - 1), 0), (0, 0)))
    cw = jnp.pad(p["conv_w"], ((0, SUBLANES - CONV_WIDTH), (0, 0)))
    ngrp = p["wa"].shape[0]
    gw = p["wa"].shape[1]

    def full(shape):
        return pl.BlockSpec(shape, lambda i, j: (0,) * len(shape))

    yb, cout, hout = pl.pallas_call(
        functools.partial(_rglru_kernel, tc=tc, seg=seg, reset_first=reset_first),
        grid=(batch // nb, t // tc),
        in_specs=[
            pl.BlockSpec((nb, tc, width), lambda i, j: (i, j, 0)),
            pl.BlockSpec((nb, tc, width), lambda i, j: (i, j, 1)),
            pl.BlockSpec((nb, SUBLANES, width), lambda i, j: (i, 0, 0)),
            pl.BlockSpec((nb, width), lambda i, j: (i, 0)),
            full((SUBLANES, width)), full((1, width)),
            full((ngrp, gw, gw)), full((ngrp, gw, gw)),
            full((1, width)), full((1, width)), full((1, width)),
        ],
        out_specs=[
            pl.BlockSpec((nb, tc, width), lambda i, j: (i, j, 0)),
            pl.BlockSpec((nb, SUBLANES, width), lambda i, j: (i, 0, 0)),
            pl.BlockSpec((nb, width), lambda i, j: (i, 0)),
        ],
        out_shape=[
            jax.ShapeDtypeStruct((batch, t, width), BF16),
            jax.ShapeDtypeStruct((batch, SUBLANES, width), F32),
            jax.ShapeDtypeStruct((batch, width), F32),
        ],
        scratch_shapes=[
            pltpu.VMEM((nb, tc + SUBLANES, width), F32),
            pltpu.VMEM((width // LANES, nb * seg, LANES), F32),
            pltpu.VMEM((width // LANES, nb * seg, LANES), F32),
            pltpu.VMEM((nb, width), F32),
        ],
        compiler_params=_params("parallel", "arbitrary"),
        name="rglru",
    )(z3, z3, c0, h0, cw, p["conv_b"], p["wa"], p["wi"], p["ba"], p["bi"], p["lam"])
    return yb, cout[:, SUBLANES - (CONV_WIDTH - 1):, :], hout


def _mix_kernel(o1, o2, o3, l1, l2, l3, yb_ref, ga_ref, gb_ref, x_ref, wpa_ref, wpb_ref, wo_ref, nf_ref,
                wr_ref, br_ref, x1_ref, xn_ref, idx_ref, gate_ref):
    la, lb, lc = l1[...], l2[...], l3[...]
    m = jnp.maximum(jnp.maximum(la, lb), lc)
    ea, eb, ec = jnp.exp(la - m), jnp.exp(lb - m), jnp.exp(lc - m)
    inv = 1.0 / (ea + eb + ec)
    att = (ea * inv) * o1[...] + (eb * inv) * o2[...] + (ec * inv) * o3[...]
    y_a = jnp.dot(att.astype(BF16), wpa_ref[...], preferred_element_type=F32)
    y_b = jnp.dot(yb_ref[...], wpb_ref[...], preferred_element_type=F32)
    merged = _sigmoid(ga_ref[...].astype(F32)) * y_a + _sigmoid(gb_ref[...].astype(F32)) * y_b
    x1 = x_ref[...] + jnp.dot(merged.astype(BF16), wo_ref[...], preferred_element_type=F32)
    x1_ref[...] = x1
    ms = jnp.mean(x1 * x1, axis=-1, keepdims=True)
    xn = (x1 * lax.rsqrt(ms + RMS_EPS) * nf_ref[...]).astype(BF16)
    xn_ref[...] = xn
    logits = jnp.dot(xn, wr_ref[...], preferred_element_type=F32) + br_ref[...]
    lane = lax.broadcasted_iota(jnp.int32, logits.shape, 1)
    lane_f = lane.astype(F32)
    logits = jnp.where(lane < N_EXPERTS, logits, NEG_INF)
    vals, idxs = [], []
    for _ in range(TOP_K):
        mk = jnp.max(logits, axis=-1, keepdims=True)
        ik = jnp.min(jnp.where(logits == mk, lane_f, float(LANES)), axis=-1, keepdims=True)
        logits = jnp.where(lane_f == ik, NEG_INF, logits)
        vals.append(mk)
        idxs.append(ik)
    es = [jnp.exp(vk - vals[0]) for vk in vals]
    tot = es[0] + es[1] + es[2] + es[3]
    idx_out = jnp.zeros(logits.shape, F32)
    gate_out = jnp.zeros(logits.shape, F32)
    for kk in range(TOP_K):
        idx_out = jnp.where(lane == kk, idxs[kk], idx_out)
        gate_out = jnp.where(lane == kk, es[kk] / tot, gate_out)
    idx_ref[...] = idx_out.astype(jnp.int32)
    gate_ref[...] = gate_out


def _mix(os_, ls_, yb, z, x2d, p):
    n, d = x2d.shape
    tm = min(256, n)

    def rows(w, c=0):
        return pl.BlockSpec((tm, w), lambda i: (i, c))

    def full(a):
        return pl.BlockSpec(a.shape, lambda i: (0,) * a.ndim)

    x1, xn, idx, gate = pl.pallas_call(
        _mix_kernel,
        grid=(n // tm,),
        in_specs=[rows(GROUP_W)] * 6 + [rows(d), rows(d, 2), rows(d, 3), rows(d),
                                        full(p["w_pa"]), full(p["w_pb"]), full(p["w_o"]), full(p["norm_ffn"]),
                                        full(p["w_router"]), full(p["b_router"])],
        out_specs=[rows(d), rows(d), rows(LANES), rows(LANES)],
        out_shape=[jax.ShapeDtypeStruct((n, d), F32), jax.ShapeDtypeStruct((n, d), BF16),
                   jax.ShapeDtypeStruct((n, LANES), jnp.int32), jax.ShapeDtypeStruct((n, LANES), F32)],
        compiler_params=_params("parallel"),
        name="mix",
    )(*os_, *ls_, yb, z, z, x2d, p["w_pa"], p["w_pb"], p["w_o"], p["norm_ffn"], p["w_router"], p["b_router"])
    return x1, xn, idx[:, :TOP_K], gate


def _expert_kernel(be_ref, nused_ref, x_ref, w1_ref, b1_ref, w2_ref, b2_ref, y_ref):
    i = pl.program_id(0)

    @pl.when(i < nused_ref[0])
    def _():
        dff = w2_ref.shape[0]
        h = jnp.dot(x_ref[...], w1_ref[...], preferred_element_type=F32) + b1_ref[...]
        x_glu = jnp.minimum(h[:, :dff], SWIGLU_LIMIT)
        x_lin = jnp.clip(h[:, dff:], -SWIGLU_LIMIT, SWIGLU_LIMIT)
        act = x_glu * _sigmoid(SWIGLU_ALPHA * x_glu) * (x_lin + 1.0)
        y = jnp.dot(act.astype(BF16), w2_ref[...], preferred_element_type=F32) + b2_ref[...]
        y_ref[...] = y.astype(y_ref.dtype)

    @pl.when(i >= nused_ref[0])
    def _():
        y_ref[...] = jnp.zeros_like(y_ref)


def _experts(x_rows, block_expert, n_used, p):
    n_rows, d = x_rows.shape
    n_blocks = n_rows // MOE_BLOCK
    dff2 = p["w1"].shape[-1]
    grid_spec = pltpu.PrefetchScalarGridSpec(
        num_scalar_prefetch=2,
        grid=(n_blocks,),
        in_specs=[
            pl.BlockSpec((MOE_BLOCK, d), lambda i, be, nu: (i, 0)),
            pl.BlockSpec((None, d, dff2), lambda i, be, nu: (be[i], 0, 0)),
            pl.BlockSpec((None, 1, dff2), lambda i, be, nu: (be[i], 0, 0)),
            pl.BlockSpec((None, dff2 // 2, d), lambda i, be, nu: (be[i], 0, 0)),
            pl.BlockSpec((None, 1, d), lambda i, be, nu: (be[i], 0, 0)),
        ],
        out_specs=pl.BlockSpec((MOE_BLOCK, d), lambda i, be, nu: (i, 0)),
    )
    return pl.pallas_call(
        _expert_kernel,
        grid_spec=grid_spec,
        out_shape=jax.ShapeDtypeStruct((n_rows, d), BF16),
        compiler_params=_params("arbitrary"),
        name="experts",
    )(block_expert, n_used, x_rows, p["w1"], p["b1"], p["w2"], p["b2"])


def _combine_kernel(yg_ref, gate_ref, x1_ref, nf_ref, o_ref):
    d = x1_ref.shape[-1]
    g = gate_ref[...]
    y = jnp.zeros(x1_ref.shape, F32)
    for kk in range(TOP_K):
        y = y + yg_ref[:, kk * d:(kk + 1) * d].astype(F32) * g[:, kk:kk + 1]
    x = x1_ref[...] + y
    ms = jnp.mean(x * x, axis=-1, keepdims=True)
    o_ref[...] = x * lax.rsqrt(ms + RMS_EPS) * nf_ref[...]


def _combine(yg, gate, x1, norm_final):
    n, d = x1.shape
    tm = min(256, n)
    return pl.pallas_call(
        _combine_kernel,
        grid=(n // tm,),
        in_specs=[pl.BlockSpec((tm, TOP_K * d), lambda i: (i, 0)),
                  pl.BlockSpec((tm, LANES), lambda i: (i, 0)),
                  pl.BlockSpec((tm, d), lambda i: (i, 0)),
                  pl.BlockSpec((1, d), lambda i: (0, 0))],
        out_specs=pl.BlockSpec((tm, d), lambda i: (i, 0)),
        out_shape=jax.ShapeDtypeStruct((n, d), F32),
        compiler_params=_params("parallel"),
        name="combine",
    )(yg, gate, x1, norm_final)


def _route(top_idx):
    n_tok = top_idx.shape[0]
    n_assign = n_tok * TOP_K
    e_flat = top_idx.reshape(-1).astype(jnp.int32)
    order = jnp.argsort(e_flat)
    onehot = (e_flat[:, None] == jnp.arange(N_EXPERTS, dtype=jnp.int32)[None, :]).astype(jnp.int32)
    csum = jnp.cumsum(onehot, axis=0)
    counts = csum[-1]
    rank = jnp.take_along_axis(csum, e_flat[:, None], axis=1)[:, 0] - 1
    padded = (counts + MOE_BLOCK - 1) // MOE_BLOCK * MOE_BLOCK
    starts = jnp.cumsum(counts) - counts
    pends = jnp.cumsum(padded)
    pstarts = pends - padded
    dest = pstarts[e_flat] + rank
    n_blocks = -(-(n_assign + N_EXPERTS * (MOE_BLOCK - 1)) // MOE_BLOCK)
    n_rows = n_blocks * MOE_BLOCK
    block_expert = jnp.minimum(
        jnp.searchsorted(pends, jnp.arange(n_blocks, dtype=jnp.int32) * MOE_BLOCK, side="right"),
        N_EXPERTS - 1).astype(jnp.int32)
    row_e = jnp.repeat(block_expert, MOE_BLOCK)
    local = jnp.arange(n_rows, dtype=jnp.int32) - pstarts[row_e]
    live = (local >= 0) & (local < counts[row_e])
    src = order[jnp.clip(starts[row_e] + local, 0, n_assign - 1)] // TOP_K
    row_tok = jnp.where(live, src, n_tok).astype(jnp.int32)
    n_used = (pends[-1] // MOE_BLOCK).astype(jnp.int32).reshape(1)
    return row_tok, dest, block_expert, n_used


def _mixer(x, p, caches, conv0, h0):
    batch, t, d = x.shape
    prompt = caches is None
    x2d = x.reshape(batch * t, d)
    z = _in_proj(x2d, p["norm_mix"], p["w_in"], BF16 if prompt else F32)
    os_, ls_, new_kv = [], [], []
    for g in range(N_GROUPS):
        if prompt:
            o, l = _attn_prompt(z, batch, t, g)
            keep = min(ATT_WINDOWS[g], t)
            z4 = z.reshape(batch, t, -1)
            kc, vc = (COL_K + g) * GROUP_W, (COL_V + g) * GROUP_W
            k_last = z4[:, t - keep:, kc:kc + GROUP_W].astype(F32).reshape(batch, keep, HEADS_PER_GROUP, HEAD_DIM)
            v_last = z4[:, t - keep:, vc:vc + GROUP_W].astype(F32).reshape(batch, keep, HEADS_PER_GROUP, HEAD_DIM)
            kv = jnp.stack([k_last, v_last], axis=2)
        else:
            o, l, kv = _attn_sample(z, caches[g], g)
        os_.append(o)
        ls_.append(l)
        new_kv.append(kv[None])
    if prompt:
        conv0 = jnp.zeros((batch, CONV_WIDTH - 1, d), F32)
        h0 = jnp.zeros((batch, d), F32)
    yb, new_conv, h_last = _rglru(z.reshape(batch, t, -1), conv0, h0, p, prompt)
    x1, xn, idx, gate = _mix(os_, ls_, yb.reshape(batch * t, d), z, x2d, p)
    return x1, xn, idx, gate, new_kv, new_conv[None], h_last[None]


def kernel(x_prompt, x_sample, cache_kv_g1, cache_kv_g2, cache_kv_g3, state_conv, state_h, norm_mix, w_in, w_pa,
           w_pb, w_o, conv_w, conv_b, lru_wa, lru_ba, lru_wi, lru_bi, lru_lambda, norm_ffn, w_router, b_router,
           w1, b1, w2, b2, norm_final):
    d = x_prompt.shape[-1]
    n_qkv = N_GROUPS * GROUP_W * 3

    def block_diag(w):
        per = MXU_DIM // LRU_BLOCK_W
        w = w.reshape(-1, per, LRU_BLOCK_W, LRU_BLOCK_W)
        eye = jnp.eye(per, dtype=w.dtype)
        return jnp.einsum("gacd,ab->gacbd", w, eye).reshape(-1, MXU_DIM, MXU_DIM).astype(BF16)

    w_in0 = w_in[0]
    dff = w2.shape[2]
    p = dict(
        norm_mix=norm_mix[0][None], norm_ffn=norm_ffn[0][None],
        w_in=jnp.concatenate([w_in0[:, n_qkv:], w_in0[:, :n_qkv]], axis=1).astype(BF16),
        w_pa=w_pa[0].astype(BF16), w_pb=w_pb[0].astype(BF16), w_o=w_o[0].astype(BF16),
        conv_w=conv_w[0], conv_b=conv_b[0][None],
        wa=block_diag(lru_wa[0]), wi=block_diag(lru_wi[0]),
        ba=lru_ba[0].reshape(1, d), bi=lru_bi[0].reshape(1, d), lam=lru_lambda[0][None],
        w_router=jnp.pad(w_router[0], ((0, 0), (0, LANES - N_EXPERTS))).astype(BF16),
        b_router=jnp.pad(b_router[0], (0, LANES - N_EXPERTS))[None],
        w1=jnp.concatenate([w1[0][:, :, 0::2], w1[0][:, :, 1::2]], axis=-1).astype(BF16),
        b1=jnp.concatenate([b1[0][:, 0::2], b1[0][:, 1::2]], axis=-1)[:, None, :],
        w2=w2[0].astype(BF16), b2=b2[0][:, None, :],
    )
    assert dff == d

    x1_p, xn_p, idx_p, gate_p, kv_p, conv_p, h_p = _mixer(x_prompt, p, None, None, None)
    x1_s, xn_s, idx_s, gate_s, kv_s, conv_s, h_s = _mixer(
        x_sample, p, [cache_kv_g1[0], cache_kv_g2[0], cache_kv_g3[0]], state_conv[0], state_h[0])

    n_p = x1_p.shape[0]
    row_tok, dest, block_expert, n_used = _route(jnp.concatenate([idx_p, idx_s], axis=0))
    xn_all = jnp.concatenate([xn_p, xn_s, jnp.zeros((1, d), BF16)], axis=0)
    y_rows = _experts(xn_all[row_tok], block_expert, n_used, p)
    yg = y_rows[dest].reshape(-1, TOP_K * d)
    nf = norm_final[None]
    y_p = _combine(yg[:n_p], gate_p, x1_p, nf).reshape(x_prompt.shape)
    y_s = _combine(yg[n_p:], gate_s, x1_s, nf).reshape(x_sample.shape)
    return (y_p, y_s, kv_p[0], kv_p[1], kv_p[2], conv_p, h_p, kv_s[0], kv_s[1], kv_s[2], conv_s, h_s)
```

```python
import functools

import numpy as np
import jax
import jax.numpy as jnp
from jax import lax
from jax.experimental import pallas as pl
from jax.experimental.pallas import tpu as pltpu

F32 = jnp.float32
BF16 = jnp.bfloat16

N_GROUPS = 3
HEADS_PER_GROUP = 8
HEAD_DIM = 64
GROUP_W = HEADS_PER_GROUP * HEAD_DIM
ATT_WINDOWS = (128, 512, 2048)
ATT_DILATIONS = (1, 4, 16)
ATT_BLOCK = 128
N_ATT_HEADS = N_GROUPS * HEADS_PER_GROUP
CONV_WIDTH = 4
LRU_C = 8.0
LRU_BLOCK_W = 64
N_EXPERTS = 32
TOP_K = 4
SWIGLU_ALPHA = 1.702
SWIGLU_LIMIT = 7.0
RMS_EPS = 1e-6
NEG_INF = float("-inf")

LANES = 128
SUBLANES = 8
MXU_DIM = 256
VMEM_LIMIT_BYTES = 56 * 1024 * 1024

COL_QKV = 8
N_NATURAL_BLOCKS = COL_QKV + 3
LANE_GROUPS = GROUP_W // LANES
MOE_BLOCK = 256
PROJ_TILE = 512

_SLOPES = [float(np.float32(2.0 ** (-8.0 * (i + 1) / N_ATT_HEADS))) for i in range(N_ATT_HEADS)]


def _params(*sem):
    return pltpu.CompilerParams(dimension_semantics=sem, vmem_limit_bytes=VMEM_LIMIT_BYTES)


def _sigmoid(x):
    return 1.0 / (1.0 + jnp.exp(-x))


def _log2(n):
    assert n > 0 and n & (n - 1) == 0, n
    return n.bit_length() - 1


def _in_proj_kernel(*refs, dilation):
    if dilation > 1:
        x_ref, g_ref, w_ref, perm_ref, o_ref, xn_ref = refs
    else:
        x_ref, g_ref, w_ref, o_ref, xn_ref = refs

    @pl.when(pl.program_id(1) == 0)
    def _():
        x = x_ref[...]
        ms = jnp.mean(x * x, axis=-1, keepdims=True)
        xn = (x * lax.rsqrt(ms + RMS_EPS) * g_ref[...]).astype(BF16)
        if dilation > 1:
            xn = jnp.dot(perm_ref[...], xn, preferred_element_type=F32).astype(BF16)
        xn_ref[...] = xn

    res = jnp.dot(xn_ref[...], w_ref[...], preferred_element_type=F32).astype(o_ref.dtype)
    if dilation > 1:
        per = res.shape[0] // dilation
        for r in range(dilation):
            o_ref[r] = res[r * per:(r + 1) * per, :]
    else:
        o_ref[...] = res


def _in_proj(x2d, gain, w, out_dtype, n_col_tiles, dilation=1, seq=None):
    n, d = x2d.shape
    cols = w.shape[1]
    tm = min(PROJ_TILE, n)
    tn = cols // n_col_tiles
    in_specs = [
        pl.BlockSpec((tm, d), lambda i, j: (i, 0)),
        pl.BlockSpec((1, d), lambda i, j: (0, 0)),
        pl.BlockSpec((d, tn), lambda i, j: (0, j)),
    ]
    args = [x2d, gain, w]
    if dilation > 1:
        per = tm // dilation
        o = np.arange(tm)
        perm = np.zeros((tm, tm), np.float32)
        perm[o, (o % per) * dilation + o // per] = 1.0
        in_specs.append(pl.BlockSpec((tm, tm), lambda i, j: (0, 0)))
        args.append(jnp.asarray(perm, BF16))
        tiles_per_seq = seq // tm
        out_spec = pl.BlockSpec((None, dilation, per, tn), lambda i, j: (i // tiles_per_seq, 0, i % tiles_per_seq, j))
        out_shape = jax.ShapeDtypeStruct((n // seq, dilation, seq // dilation, cols), out_dtype)
    else:
        out_spec = pl.BlockSpec((tm, tn), lambda i, j: (i, j))
        out_shape = jax.ShapeDtypeStruct((n, cols), out_dtype)
    return pl.pallas_call(
        functools.partial(_in_proj_kernel, dilation=dilation),
        grid=(n // tm, n_col_tiles),
        in_specs=in_specs,
        out_specs=out_spec,
        out_shape=out_shape,
        scratch_shapes=[pltpu.VMEM((tm, d), BF16)],
        compiler_params=_params("parallel", "arbitrary"),
        name=f"in_proj_d{dilation}",
    )(*args)


def _attn_prompt_kernel(q_ref, kp_ref, kc_ref, vp_ref, vc_ref, o_ref, l_ref, *, slopes, dilation):
    blk = ATT_BLOCK
    j = pl.program_id(1)
    qq = lax.broadcasted_iota(jnp.int32, (blk, 2 * blk), 0)
    kk = lax.broadcasted_iota(jnp.int32, (blk, 2 * blk), 1)
    step = qq + blk - kk
    valid = (step >= 0) & (step <= blk) & ((kk >= blk) | (j > 0))
    dist = (step * dilation).astype(F32)
    low = lax.broadcasted_iota(jnp.int32, (blk, LANES), 1) < HEAD_DIM

    def residue(r, carry):
        q = q_ref[r] * (HEAD_DIM ** -0.5)
        k = jnp.concatenate([kp_ref[r], kc_ref[r]], axis=0)
        v = jnp.concatenate([vp_ref[r], vc_ref[r]], axis=0)
        rows = pl.ds(r, blk, stride=dilation) if dilation > 1 else slice(None)
        for p in range(LANE_GROUPS):
            cs = slice(p * LANES, (p + 1) * LANES)
            qg, kg, vg = q[:, cs], k[:, cs], v[:, cs]
            outs, lses = [], []
            for hh in range(2):
                sel = low if hh == 0 else jnp.logical_not(low)
                qm = jnp.where(sel, qg, jnp.zeros_like(qg))
                s = lax.dot_general(qm, kg, (((1,), (1,)), ((), ())), preferred_element_type=F32)
                s = s - slopes[2 * p + hh] * dist
                s = jnp.where(valid, s, NEG_INF)
                m = jnp.max(s, axis=-1, keepdims=True)
                e = jnp.exp(s - m)
                den = jnp.sum(e, axis=-1, keepdims=True)
                outs.append(jnp.dot(e.astype(BF16), vg, preferred_element_type=F32) / den)
                lses.append(m + jnp.log(den))
            o_ref[p, rows, :] = jnp.where(low, outs[0], outs[1])
            l_ref[p, rows, :] = jnp.where(low, lses[0], lses[1])
        return carry

    if dilation > 1:
        lax.fori_loop(0, dilation, residue, 0)
    else:
        residue(0, 0)


def _attn_prompt(zq, g, col0):
    d = ATT_DILATIONS[g]
    batch, _, n, _ = zq.shape
    nb = n // ATT_BLOCK
    blk = (None, d, ATT_BLOCK, GROUP_W)

    def cur(c):
        return pl.BlockSpec(blk, lambda b, j: (b, 0, j, c))

    def prev(c):
        return pl.BlockSpec(blk, lambda b, j: (b, 0, jnp.maximum(j - 1, 0), c))

    out_spec = pl.BlockSpec((LANE_GROUPS, ATT_BLOCK * d, LANES), lambda b, j: (0, b * nb + j, 0))
    out_sds = jax.ShapeDtypeStruct((LANE_GROUPS, batch * n * d, LANES), F32)
    kern = functools.partial(_attn_prompt_kernel, slopes=_SLOPES[g * HEADS_PER_GROUP:(g + 1) * HEADS_PER_GROUP],
                             dilation=d)
    return pl.pallas_call(
        kern,
        grid=(batch, nb),
        in_specs=[cur(col0), prev(col0 + 1), cur(col0 + 1), prev(col0 + 2), cur(col0 + 2)],
        out_specs=[out_spec, out_spec],
        out_shape=[out_sds, out_sds],
        compiler_params=_params("parallel", "arbitrary"),
        name=f"attn_prompt_g{g}",
    )(zq, zq, zq, zq, zq)


def _attn_sample_kernel(q_ref, k_ref, v_ref, c_ref, sl_ref, o_ref, l_ref, cout_ref, *, window, dilation):
    t_new = q_ref.shape[0]
    rows = HEADS_PER_GROUP * t_new
    q = q_ref[...] * (HEAD_DIM ** -0.5)
    pad = jnp.zeros((LANES - t_new, GROUP_W), F32)
    kn = jnp.concatenate([k_ref[...], pad], axis=0)
    vn = jnp.concatenate([v_ref[...], pad], axis=0)
    new_t = jnp.concatenate([kn.T, vn.T], axis=0)
    cin = c_ref[...]
    cout_ref[...] = pltpu.roll(cin, window - t_new, axis=1)
    tail = cout_ref[:, window - LANES:window]
    lane = lax.broadcasted_iota(jnp.int32, tail.shape, 1)
    cout_ref[:, window - LANES:window] = jnp.where(lane >= LANES - t_new,
                                                   pltpu.roll(new_t, LANES - t_new, axis=1), tail)
    qt = jnp.concatenate([q] * HEADS_PER_GROUP, axis=0)
    row_h = lax.broadcasted_iota(jnp.int32, (rows, GROUP_W), 0) >> _log2(t_new)
    col_h = lax.broadcasted_iota(jnp.int32, (rows, GROUP_W), 1) >> _log2(HEAD_DIM)
    qbd = jnp.where(row_h == col_h, qt, 0.0).astype(BF16)
    k_t = cin[0:GROUP_W, :].astype(BF16)
    v_t = cin[GROUP_W:2 * GROUP_W, :].astype(BF16)
    s_c = jnp.dot(qbd, k_t, preferred_element_type=F32)
    s_n = jnp.dot(qbd, new_t[0:GROUP_W, :].astype(BF16), preferred_element_type=F32)
    _log2(dilation)
    slope = sl_ref[:, 0:1]
    t_c = lax.broadcasted_iota(jnp.int32, s_c.shape, 0) & (t_new - 1)
    dist_c = window + t_c - lax.broadcasted_iota(jnp.int32, s_c.shape, 1)
    s_c = jnp.where((dist_c <= window) & ((dist_c & (dilation - 1)) == 0), s_c - slope * dist_c.astype(F32), NEG_INF)
    t_n = lax.broadcasted_iota(jnp.int32, s_n.shape, 0) & (t_new - 1)
    dist_n = t_n - lax.broadcasted_iota(jnp.int32, s_n.shape, 1)
    s_n = jnp.where((dist_n >= 0) & ((dist_n & (dilation - 1)) == 0), s_n - slope * dist_n.astype(F32), NEG_INF)
    m = jnp.maximum(jnp.max(s_c, axis=-1, keepdims=True), jnp.max(s_n, axis=-1, keepdims=True))
    e_c = jnp.exp(s_c - m)
    e_n = jnp.exp(s_n - m)
    den = jnp.sum(e_c, axis=-1, keepdims=True) + jnp.sum(e_n, axis=-1, keepdims=True)
    o_full = lax.dot_general(e_c.astype(BF16), v_t, (((1,), (1,)), ((), ())), preferred_element_type=F32)
    o_full = (o_full + jnp.dot(e_n.astype(BF16), vn.astype(BF16), preferred_element_type=F32)) / den
    lse = m + jnp.log(den)
    out_h = lax.broadcasted_iota(jnp.int32, (t_new, GROUP_W), 1) >> _log2(HEAD_DIM)
    o = jnp.zeros((t_new, GROUP_W), F32)
    l = jnp.zeros((t_new, GROUP_W), F32)
    for h in range(HEADS_PER_GROUP):
        rs = slice(h * t_new, (h + 1) * t_new)
        o = jnp.where(out_h == h, o_full[rs, :], o)
        l = jnp.where(out_h == h, lse[rs, :], l)
    for c in range(LANE_GROUPS):
        o_ref[c] = o[:, c * LANES:(c + 1) * LANES]
        l_ref[c] = l[:, c * LANES:(c + 1) * LANES]


def _attn_sample(z, cache, g):
    batch, window = cache.shape[0], cache.shape[1]
    n_tok = z.shape[0]
    t_new = n_tok // batch
    d = ATT_DILATIONS[g]
    c_t = jnp.transpose(cache, (0, 2, 3, 4, 1)).reshape(batch, 2 * GROUP_W, window)
    slopes = np.repeat(np.asarray(_SLOPES[g * HEADS_PER_GROUP:(g + 1) * HEADS_PER_GROUP], np.float32), t_new)
    slopes = jnp.asarray(np.broadcast_to(slopes[:, None], (HEADS_PER_GROUP * t_new, LANES)))
    col0 = COL_QKV + 3 * g

    def col(c):
        return pl.BlockSpec((t_new, GROUP_W), lambda b: (b, c))

    tok_spec = pl.BlockSpec((LANE_GROUPS, t_new, LANES), lambda b: (0, b, 0))
    tok_sds = jax.ShapeDtypeStruct((LANE_GROUPS, n_tok, LANES), F32)
    buf_spec = pl.BlockSpec((None, 2 * GROUP_W, window), lambda b: (b, 0, 0))
    o, l, cout = pl.pallas_call(
        functools.partial(_attn_sample_kernel, window=window, dilation=d),
        grid=(batch,),
        in_specs=[col(col0), col(col0 + 1), col(col0 + 2), buf_spec, pl.BlockSpec(slopes.shape, lambda b: (0, 0))],
        out_specs=[tok_spec, tok_spec, buf_spec],
        out_shape=[tok_sds, tok_sds, jax.ShapeDtypeStruct(c_t.shape, F32)],
        compiler_params=_params("parallel"),
        name=f"attn_sample_g{g}",
    )(z, z, z, c_t, slopes)
    cout = cout.reshape(batch, 2, HEADS_PER_GROUP, HEAD_DIM, window)
    return o, l, jnp.transpose(cout, (0, 4, 1, 2, 3))


def _gelu_tanh(x):
    return x * (0.5 * (1.0 + jnp.tanh(np.sqrt(2.0 / np.pi).astype(np.float32) * (x + 0.044715 * (x * x * x)))))


def _rglru_kernel(xb_ref, yg_ref, c0_ref, h0_ref, cw_ref, cb_ref, wa_ref, wi_ref, ba_ref, bi_ref, lam_ref,
                  yb_ref, cout_ref, hout_ref, xpad, a_s, u_s, h_s, *, tc, seg, reset_first):
    nb = xb_ref.shape[0]
    ngl = a_s.shape[0]
    j = pl.program_id(1)

    @pl.when(j == 0)
    def _():
        xpad[:, 0:SUBLANES, :] = c0_ref[...]
        h_s[...] = h0_ref[...]

    @pl.when(j > 0)
    def _():
        xpad[:, 0:SUBLANES, :] = xpad[:, tc:tc + SUBLANES, :]

    xpad[:, SUBLANES:SUBLANES + tc, :] = xb_ref[...].astype(F32)
    cout_ref[...] = xpad[:, tc:tc + SUBLANES, :]

    lam = lam_ref[...]
    softplus_neg = jnp.maximum(-lam, 0.0) + jnp.log1p(jnp.exp(-jnp.abs(lam)))
    first = (lax.broadcasted_iota(jnp.int32, (tc, 1), 0) == 0) & (j == 0)
    gw = wa_ref.shape[1]
    for b in range(nb):
        xc = cb_ref[...]
        for tap in range(CONV_WIDTH):
            off = SUBLANES - (CONV_WIDTH - 1) + tap
            xc = xc + xpad[b, off:off + tc, :] * cw_ref[tap:tap + 1, :]
        xcb = xc.astype(BF16)
        ra, ri = [], []
        for blk in range(wa_ref.shape[0]):
            xs = xcb[:, blk * gw:(blk + 1) * gw]
            ra.append(jnp.dot(xs, wa_ref[blk], preferred_element_type=F32))
            ri.append(jnp.dot(xs, wi_ref[blk], preferred_element_type=F32))
        r = _sigmoid(jnp.concatenate(ra, axis=-1) + ba_ref[...])
        i = _sigmoid(jnp.concatenate(ri, axis=-1) + bi_ref[...])
        log_a = -LRU_C * r * softplus_neg
        a = jnp.exp(log_a)
        mult = jnp.sqrt(1.0 - jnp.exp(2.0 * log_a))
        if reset_first:
            mult = jnp.where(first, 1.0, mult)
        u = mult * (i * xc)
        for c in range(ngl):
            a_s[c, b * seg:b * seg + tc, :] = a[:, c * LANES:(c + 1) * LANES]
            u_s[c, b * seg:b * seg + tc, :] = u[:, c * LANES:(c + 1) * LANES]

    def step(t, hs):
        idx = pl.ds(t, nb, stride=seg)
        out = []
        for c in range(ngl):
            h = a_s[c, idx, :] * hs[c] + u_s[c, idx, :]
            u_s[c, idx, :] = h
            out.append(h)
        return tuple(out)

    hs = lax.fori_loop(0, tc, step, tuple(h_s[:, c * LANES:(c + 1) * LANES] for c in range(ngl)))
    h = jnp.concatenate(hs, axis=-1)
    h_s[...] = h
    hout_ref[...] = h
    for b in range(nb):
        hb = jnp.concatenate([u_s[c, b * seg:b * seg + tc, :] for c in range(ngl)], axis=-1)
        yb_ref[b] = (_gelu_tanh(yg_ref[b].astype(F32)) * hb).astype(yb_ref.dtype)


def _rglru(z3, conv0, h0, p, reset_first):
    batch, t, _ = z3.shape
    width = h0.shape[-1]
    nb = SUBLANES
    tc = min(128, t)
    seg = tc + SUBLANES
    c0 = jnp.pad(conv0, ((0, 0), (SUBLANES - (CONV_WIDTH - 1), 0), (0, 0)))
    cw = jnp.pad(p["conv_w"], ((0, SUBLANES - CONV_WIDTH), (0, 0)))
    ngrp = p["wa"].shape[0]
    gw = p["wa"].shape[1]

    def full(shape):
        return pl.BlockSpec(shape, lambda i, j: (0,) * len(shape))

    yb, cout, hout = pl.pallas_call(
        functools.partial(_rglru_kernel, tc=tc, seg=seg, reset_first=reset_first),
        grid=(batch // nb, t // tc),
        in_specs=[
            pl.BlockSpec((nb, tc, width), lambda i, j: (i, j, 0)),
            pl.BlockSpec((nb, tc, width), lambda i, j: (i, j, 1)),
            pl.BlockSpec((nb, SUBLANES, width), lambda i, j: (i, 0, 0)),
            pl.BlockSpec((nb, width), lambda i, j: (i, 0)),
            full((SUBLANES, width)), full((1, width)),
            full((ngrp, gw, gw)), full((ngrp, gw, gw)),
            full((1, width)), full((1, width)), full((1, width)),
        ],
        out_specs=[
            pl.BlockSpec((nb, tc, width), lambda i, j: (i, j, 0)),
            pl.BlockSpec((nb, SUBLANES, width), lambda i, j: (i, 0, 0)),
            pl.BlockSpec((nb, width), lambda i, j: (i, 0)),
        ],
        out_shape=[
            jax.ShapeDtypeStruct((batch, t, width), BF16),
            jax.ShapeDtypeStruct((batch, SUBLANES, width), F32),
            jax.ShapeDtypeStruct((batch, width), F32),
        ],
        scratch_shapes=[
            pltpu.VMEM((nb, tc + SUBLANES, width), F32),
            pltpu.VMEM((width // LANES, nb * seg, LANES), F32),
            pltpu.VMEM((width // LANES, nb * seg, LANES), F32),
            pltpu.VMEM((nb, width), F32),
        ],
        compiler_params=_params("parallel", "arbitrary"),
        name="rglru",
    )(z3, z3, c0, h0, cw, p["conv_b"], p["wa"], p["wi"], p["ba"], p["bi"], p["lam"])
    return yb, cout[:, SUBLANES - (CONV_WIDTH - 1):, :], hout


def _mix_kernel(o1, o2, o3, l1, l2, l3, yb_ref, ga_ref, gb_ref, x_ref, wpa_ref, wpb_ref, wo_ref, nf_ref,
                wr_ref, br_ref, x1_ref, xn_ref, idx_ref, gate_ref):
    parts = []
    for c in range(LANE_GROUPS):
        la, lb, lc = l1[c], l2[c], l3[c]
        m = jnp.maximum(jnp.maximum(la, lb), lc)
        ea, eb, ec = jnp.exp(la - m), jnp.exp(lb - m), jnp.exp(lc - m)
        inv = 1.0 / (ea + eb + ec)
        parts.append((ea * inv) * o1[c] + (eb * inv) * o2[c] + (ec * inv) * o3[c])
    att = jnp.concatenate(parts, axis=-1)
    y_a = jnp.dot(att.astype(BF16), wpa_ref[...], preferred_element_type=F32)
    y_b = jnp.dot(yb_ref[...], wpb_ref[...], preferred_element_type=F32)
    merged = _sigmoid(ga_ref[...].astype(F32)) * y_a + _sigmoid(gb_ref[...].astype(F32)) * y_b
    x1 = x_ref[...] + jnp.dot(merged.astype(BF16), wo_ref[...], preferred_element_type=F32)
    x1_ref[...] = x1
    ms = jnp.mean(x1 * x1, axis=-1, keepdims=True)
    xn = (x1 * lax.rsqrt(ms + RMS_EPS) * nf_ref[...]).astype(BF16)
    xn_ref[...] = xn
    logits = jnp.dot(xn, wr_ref[...], preferred_element_type=F32) + br_ref[...]
    lane = lax.broadcasted_iota(jnp.int32, logits.shape, 1)
    lane_f = lane.astype(F32)
    logits = jnp.where(lane < N_EXPERTS, logits, NEG_INF)
    vals, idxs = [], []
    for _ in range(TOP_K):
        mk = jnp.max(logits, axis=-1, keepdims=True)
        ik = jnp.min(jnp.where(logits == mk, lane_f, float(LANES)), axis=-1, keepdims=True)
        logits = jnp.where(lane_f == ik, NEG_INF, logits)
        vals.append(mk)
        idxs.append(ik)
    es = [jnp.exp(vk - vals[0]) for vk in vals]
    tot = es[0] + es[1] + es[2] + es[3]
    idx_out = jnp.zeros(logits.shape, F32)
    gate_out = jnp.zeros(logits.shape, F32)
    for kk in range(TOP_K):
        idx_out = jnp.where(lane == kk, idxs[kk], idx_out)
        gate_out = jnp.where(lane == kk, es[kk] / tot, gate_out)
    idx_ref[...] = idx_out.astype(jnp.int32)
    gate_ref[...] = gate_out


def _mix(os_, ls_, yb, z, x2d, p):
    n, d = x2d.shape
    tm = min(256, n)

    def rows(w, c=0):
        return pl.BlockSpec((tm, w), lambda i: (i, c))

    def full(a):
        return pl.BlockSpec(a.shape, lambda i: (0,) * a.ndim)

    grp = pl.BlockSpec((LANE_GROUPS, tm, LANES), lambda i: (0, i, 0))
    x1, xn, idx, gate = pl.pallas_call(
        _mix_kernel,
        grid=(n // tm,),
        in_specs=[grp] * 6 + [rows(d), rows(d, 2), rows(d, 3), rows(d),
                              full(p["w_pa"]), full(p["w_pb"]), full(p["w_o"]), full(p["norm_ffn"]),
                              full(p["w_router"]), full(p["b_router"])],
        out_specs=[rows(d), rows(d), rows(LANES), rows(LANES)],
        out_shape=[jax.ShapeDtypeStruct((n, d), F32), jax.ShapeDtypeStruct((n, d), BF16),
                   jax.ShapeDtypeStruct((n, LANES), jnp.int32), jax.ShapeDtypeStruct((n, LANES), F32)],
        compiler_params=_params("parallel"),
        name="mix",
    )(*os_, *ls_, yb, z, z, x2d, p["w_pa"], p["w_pb"], p["w_o"], p["norm_ffn"], p["w_router"], p["b_router"])
    return x1, xn, idx[:, :TOP_K], gate


def _expert_kernel(be_ref, nused_ref, x_ref, w1_ref, b1_ref, w2_ref, b2_ref, perm_ref, y_ref, w1_s, w2_s):
    i = pl.program_id(0)
    dff = w2_ref.shape[0]
    n_cb = w1_ref.shape[1] // MXU_DIM
    half = MXU_DIM // 2

    @pl.when((i == 0) | (be_ref[i] != be_ref[jnp.maximum(i - 1, 0)]))
    def _():
        for cb in range(n_cb):
            cs = slice(cb * MXU_DIM, (cb + 1) * MXU_DIM)
            w1_s[:, cs] = jnp.dot(w1_ref[:, cs].astype(BF16), perm_ref[...], preferred_element_type=F32).astype(BF16)
        w2_s[...] = w2_ref[...].astype(BF16)

    @pl.when(i < nused_ref[0])
    def _():
        h = jnp.dot(x_ref[...], w1_s[...], preferred_element_type=F32) + b1_ref[...]
        acts = []
        for cb in range(n_cb):
            x_glu = jnp.minimum(h[:, cb * MXU_DIM:cb * MXU_DIM + half], SWIGLU_LIMIT)
            x_lin = jnp.clip(h[:, cb * MXU_DIM + half:(cb + 1) * MXU_DIM], -SWIGLU_LIMIT, SWIGLU_LIMIT)
            acts.append(x_glu * _sigmoid(SWIGLU_ALPHA * x_glu) * (x_lin + 1.0))
        act = jnp.concatenate(acts, axis=-1)
        assert act.shape[-1] == dff
        y = jnp.dot(act.astype(BF16), w2_s[...], preferred_element_type=F32) + b2_ref[...]
        y_ref[...] = y.astype(y_ref.dtype)

    @pl.when(i >= nused_ref[0])
    def _():
        y_ref[...] = jnp.zeros_like(y_ref)


def _experts(x_rows, block_expert, n_used, p):
    n_rows, d = x_rows.shape
    n_blocks = n_rows // MOE_BLOCK
    dff2 = p["w1"].shape[-1]
    half = MXU_DIM // 2
    o = np.arange(MXU_DIM)
    perm = np.zeros((MXU_DIM, MXU_DIM), np.float32)
    perm[np.where(o < half, 2 * o, 2 * (o - half) + 1), o] = 1.0
    grid_spec = pltpu.PrefetchScalarGridSpec(
        num_scalar_prefetch=2,
        grid=(n_blocks,),
        in_specs=[
            pl.BlockSpec((MOE_BLOCK, d), lambda i, be, nu: (i, 0)),
            pl.BlockSpec((None, d, dff2), lambda i, be, nu: (be[i], 0, 0)),
            pl.BlockSpec((None, 1, dff2), lambda i, be, nu: (be[i], 0, 0)),
            pl.BlockSpec((None, dff2 // 2, d), lambda i, be, nu: (be[i], 0, 0)),
            pl.BlockSpec((None, 1, d), lambda i, be, nu: (be[i], 0, 0)),
            pl.BlockSpec((MXU_DIM, MXU_DIM), lambda i, be, nu: (0, 0)),
        ],
        out_specs=pl.BlockSpec((MOE_BLOCK, d), lambda i, be, nu: (i, 0)),
        scratch_shapes=[pltpu.VMEM((d, dff2), BF16), pltpu.VMEM((dff2 // 2, d), BF16)],
    )
    return pl.pallas_call(
        _expert_kernel,
        grid_spec=grid_spec,
        out_shape=jax.ShapeDtypeStruct((n_rows, d), BF16),
        compiler_params=_params("arbitrary"),
        name="experts",
    )(block_expert, n_used, x_rows, p["w1"], p["b1"], p["w2"], p["b2"], jnp.asarray(perm, BF16))


def _combine_kernel(y0, y1, y2, y3, gate_ref, x1_ref, nf_ref, o_ref):
    g = gate_ref[...]
    y = jnp.zeros(x1_ref.shape, F32)
    for kk, y_ref in enumerate((y0, y1, y2, y3)):
        y = y + y_ref[...].astype(F32) * g[:, kk:kk + 1]
    x = x1_ref[...] + y
    ms = jnp.mean(x * x, axis=-1, keepdims=True)
    o_ref[...] = x * lax.rsqrt(ms + RMS_EPS) * nf_ref[...]


def _combine(ys, gate, x1, norm_final):
    n, d = x1.shape
    tm = min(256, n)
    row = pl.BlockSpec((tm, d), lambda i: (i, 0))
    return pl.pallas_call(
        _combine_kernel,
        grid=(n // tm,),
        in_specs=[row] * TOP_K + [pl.BlockSpec((tm, LANES), lambda i: (i, 0)), row,
                                  pl.BlockSpec((1, d), lambda i: (0, 0))],
        out_specs=row,
        out_shape=jax.ShapeDtypeStruct((n, d), F32),
        compiler_params=_params("parallel"),
        name="combine",
    )(*ys, gate, x1, norm_final)


def _route(top_idx):
    n_tok = top_idx.shape[0]
    n_assign = n_tok * TOP_K
    e_flat = top_idx.reshape(-1).astype(jnp.int32)
    order = jnp.argsort(e_flat)
    onehot = (e_flat[:, None] == jnp.arange(N_EXPERTS, dtype=jnp.int32)[None, :]).astype(jnp.int32)
    csum = jnp.cumsum(onehot, axis=0)
    counts = csum[-1]
    rank = jnp.take_along_axis(csum, e_flat[:, None], axis=1)[:, 0] - 1
    padded = (counts + MOE_BLOCK - 1) // MOE_BLOCK * MOE_BLOCK
    starts = jnp.cumsum(counts) - counts
    pends = jnp.cumsum(padded)
    pstarts = pends - padded
    dest = pstarts[e_flat] + rank
    n_blocks = -(-(n_assign + N_EXPERTS * (MOE_BLOCK - 1)) // MOE_BLOCK)
    n_rows = n_blocks * MOE_BLOCK
    block_expert = jnp.minimum(
        jnp.searchsorted(pends, jnp.arange(n_blocks, dtype=jnp.int32) * MOE_BLOCK, side="right"),
        N_EXPERTS - 1).astype(jnp.int32)
    row_e = jnp.repeat(block_expert, MOE_BLOCK)
    local = jnp.arange(n_rows, dtype=jnp.int32) - pstarts[row_e]
    live = (local >= 0) & (local < counts[row_e])
    src = order[jnp.clip(starts[row_e] + local, 0, n_assign - 1)] // TOP_K
    row_tok = jnp.where(live, src, n_tok).astype(jnp.int32)
    n_used = (pends[-1] // MOE_BLOCK).astype(jnp.int32).reshape(1)
    return row_tok, dest.reshape(n_tok, TOP_K), block_expert, n_used


def _mixer(x, p, caches, conv0, h0):
    batch, t, d = x.shape
    prompt = caches is None
    x2d = x.reshape(batch * t, d)
    os_, ls_, new_kv = [], [], []
    if prompt:
        nat = N_NATURAL_BLOCKS * GROUP_W
        z = _in_proj(x2d, p["norm_mix"], p["w_in"][:, :nat], BF16, 4)
        zq = [z.reshape(batch, 1, t, nat)]
        for g in range(1, N_GROUPS):
            c0 = (COL_QKV + 3 * g) * GROUP_W
            zq.append(_in_proj(x2d, p["norm_mix"], p["w_in"][:, c0:c0 + 3 * GROUP_W], BF16, 1,
                               dilation=ATT_DILATIONS[g], seq=t))
        for g in range(N_GROUPS):
            dil = ATT_DILATIONS[g]
            o, l = _attn_prompt(zq[g], g, COL_QKV if g == 0 else 0)
            keep = min(ATT_WINDOWS[g], t)
            if g == 0:
                kcol = (COL_QKV + 1) * GROUP_W
                last = z.reshape(batch, t, nat)[:, t - keep:, kcol:kcol + 2 * GROUP_W]
            else:
                last = zq[g][:, :, (t - keep) // dil:, GROUP_W:3 * GROUP_W]
                last = jnp.swapaxes(last, 1, 2).reshape(batch, keep, 2 * GROUP_W)
            kv = last.astype(F32).reshape(batch, keep, 2, HEADS_PER_GROUP, HEAD_DIM)
            os_.append(o)
            ls_.append(l)
            new_kv.append(kv[None])
        conv0 = jnp.zeros((batch, CONV_WIDTH - 1, d), F32)
        h0 = jnp.zeros((batch, d), F32)
    else:
        z = _in_proj(x2d, p["norm_mix"], p["w_in"], F32, 4)
        for g in range(N_GROUPS):
            o, l, kv = _attn_sample(z, caches[g], g)
            os_.append(o)
            ls_.append(l)
            new_kv.append(kv[None])
    yb, new_conv, h_last = _rglru(z.reshape(batch, t, -1), conv0, h0, p, prompt)
    x1, xn, idx, gate = _mix(os_, ls_, yb.reshape(batch * t, d), z, x2d, p)
    return x1, xn, idx, gate, new_kv, new_conv[None], h_last[None]


def kernel(x_prompt, x_sample, cache_kv_g1, cache_kv_g2, cache_kv_g3, state_conv, state_h, norm_mix, w_in, w_pa,
           w_pb, w_o, conv_w, conv_b, lru_wa, lru_ba, lru_wi, lru_bi, lru_lambda, norm_ffn, w_router, b_router,
           w1, b1, w2, b2, norm_final):
    d = x_prompt.shape[-1]
    gsz = N_GROUPS * GROUP_W

    def block_diag(w):
        per = MXU_DIM // LRU_BLOCK_W
        w = w.reshape(-1, per, LRU_BLOCK_W, LRU_BLOCK_W)
        eye = jnp.eye(per, dtype=w.dtype)
        return jnp.einsum("gacd,ab->gacbd", w, eye).reshape(-1, MXU_DIM, MXU_DIM).astype(BF16)

    w_in0 = w_in[0]
    qkv = [w_in0[:, which * gsz + g * GROUP_W: which * gsz + (g + 1) * GROUP_W]
           for g in range(N_GROUPS) for which in range(3)]
    n_e, dff2 = b1.shape[1], b1.shape[2]
    half = MXU_DIM // 2
    p = dict(
        norm_mix=norm_mix[0][None], norm_ffn=norm_ffn[0][None],
        w_in=jnp.concatenate([w_in0[:, 3 * gsz:]] + qkv, axis=1).astype(BF16),
        w_pa=w_pa[0].astype(BF16), w_pb=w_pb[0].astype(BF16), w_o=w_o[0].astype(BF16),
        conv_w=conv_w[0], conv_b=conv_b[0][None],
        wa=block_diag(lru_wa[0]), wi=block_diag(lru_wi[0]),
        ba=lru_ba[0].reshape(1, d), bi=lru_bi[0].reshape(1, d), lam=lru_lambda[0][None],
        w_router=jnp.pad(w_router[0], ((0, 0), (0, LANES - N_EXPERTS))).astype(BF16),
        b_router=jnp.pad(b_router[0], (0, LANES - N_EXPERTS))[None],
        w1=w1[0], w2=w2[0],
        b1=b1[0].reshape(n_e, dff2 // MXU_DIM, half, 2).transpose(0, 1, 3, 2).reshape(n_e, 1, dff2),
        b2=b2[0][:, None, :],
    )

    x1_p, xn_p, idx_p, gate_p, kv_p, conv_p, h_p = _mixer(x_prompt, p, None, None, None)
    x1_s, xn_s, idx_s, gate_s, kv_s, conv_s, h_s = _mixer(
        x_sample, p, [cache_kv_g1[0], cache_kv_g2[0], cache_kv_g3[0]], state_conv[0], state_h[0])

    n_p = x1_p.shape[0]
    row_tok, dest, block_expert, n_used = _route(jnp.concatenate([idx_p, idx_s], axis=0))
    xn_all = jnp.concatenate([xn_p, xn_s, jnp.zeros((1, d), BF16)], axis=0)
    y_rows = _experts(xn_all[row_tok], block_expert, n_used, p)
    nf = norm_final[None]
    y_p = _combine([y_rows[dest[:n_p, kk]] for kk in range(TOP_K)], gate_p, x1_p, nf).reshape(x_prompt.shape)
    y_s = _combine([y_rows[dest[n_p:, kk]] for kk in range(TOP_K)], gate_s, x1_s, nf).reshape(x_sample.shape)
    return (y_p, y_s, kv_p[0], kv_p[1], kv_p[2], conv_p, h_p, kv_s[0], kv_s[1], kv_s[2], conv_s, h_s)
```

```python
import functools

import numpy as np
import jax
import jax.numpy as jnp
from jax import lax
from jax.experimental import pallas as pl
from jax.experimental.pallas import tpu as pltpu

F32 = jnp.float32
BF16 = jnp.bfloat16

N_GROUPS = 3
HEADS_PER_GROUP = 8
HEAD_DIM = 64
GROUP_W = HEADS_PER_GROUP * HEAD_DIM
ATT_WINDOWS = (128, 512, 2048)
ATT_DILATIONS = (1, 4, 16)
ATT_BLOCK = 128
N_ATT_HEADS = N_GROUPS * HEADS_PER_GROUP
CONV_WIDTH = 4
LRU_C = 8.0
LRU_BLOCK_W = 64
N_EXPERTS = 32
TOP_K = 4
SWIGLU_ALPHA = 1.702
SWIGLU_LIMIT = 7.0
RMS_EPS = 1e-6
NEG_INF = float("-inf")

LANES = 128
SUBLANES = 8
MXU_DIM = 256
VMEM_LIMIT_BYTES = 56 * 1024 * 1024

COL_QKV = 8
N_NATURAL_BLOCKS = COL_QKV + 3
LANE_GROUPS = GROUP_W // LANES
MOE_BLOCK = 512
PROJ_TILE = 512

_SLOPES = [float(np.float32(2.0 ** (-8.0 * (i + 1) / N_ATT_HEADS))) for i in range(N_ATT_HEADS)]


def _params(*sem):
    return pltpu.CompilerParams(dimension_semantics=sem, vmem_limit_bytes=VMEM_LIMIT_BYTES)


def _sigmoid(x):
    return 0.5 * jnp.tanh(0.5 * x) + 0.5


def _log2(n):
    assert n > 0 and n & (n - 1) == 0, n
    return n.bit_length() - 1


def _rmsnorm_bf16(x, gain):
    ms = jnp.mean(x * x, axis=-1, keepdims=True)
    return (x * lax.rsqrt(ms + RMS_EPS) * gain).astype(BF16)


def _in_proj_kernel(x_ref, g_ref, w_ref, o_ref, xn_ref):
    @pl.when(pl.program_id(1) == 0)
    def _():
        xn_ref[...] = _rmsnorm_bf16(x_ref[...], g_ref[...])

    o_ref[...] = jnp.dot(xn_ref[...], w_ref[...], preferred_element_type=F32).astype(o_ref.dtype)


def _in_proj(x2d, gain, w, out_dtype, n_col_tiles):
    n, d = x2d.shape
    cols = w.shape[1]
    tm = min(PROJ_TILE, n)
    tn = cols // n_col_tiles
    return pl.pallas_call(
        _in_proj_kernel,
        grid=(n // tm, n_col_tiles),
        in_specs=[
            pl.BlockSpec((tm, d), lambda i, j: (i, 0)),
            pl.BlockSpec((1, d), lambda i, j: (0, 0)),
            pl.BlockSpec((d, tn), lambda i, j: (0, j)),
        ],
        out_specs=pl.BlockSpec((tm, tn), lambda i, j: (i, j)),
        out_shape=jax.ShapeDtypeStruct((n, cols), out_dtype),
        scratch_shapes=[pltpu.VMEM((tm, d), BF16)],
        compiler_params=_params("parallel", "arbitrary"),
        name="in_proj",
    )(x2d, gain, w)


def _in_proj_prompt_kernel(x_ref, g_ref, w_ref, *refs, dilations, n_chunks):
    n_dil = len(dilations)
    perm_refs, zn_ref, zd_refs = refs[:n_dil], refs[n_dil], refs[n_dil + 1:]
    xn = _rmsnorm_bf16(x_ref[...], g_ref[...])
    nat = zn_ref.shape[1]
    cw = nat // n_chunks
    for c in range(n_chunks):
        cs = slice(c * cw, (c + 1) * cw)
        zn_ref[:, cs] = jnp.dot(xn, w_ref[:, cs], preferred_element_type=F32).astype(zn_ref.dtype)
    col = nat
    for perm_ref, zd_ref, dil in zip(perm_refs, zd_refs, dilations):
        width = zd_ref.shape[-1]
        xp = jnp.dot(perm_ref[...], xn, preferred_element_type=F32).astype(BF16)
        res = jnp.dot(xp, w_ref[:, col:col + width], preferred_element_type=F32).astype(zd_ref.dtype)
        per = res.shape[0] // dil
        for r in range(dil):
            zd_ref[r] = res[r * per:(r + 1) * per, :]
        col += width


def _in_proj_prompt(x2d, gain, w, seq, nat, dilations):
    n, d = x2d.shape
    cols = w.shape[1]
    tm = PROJ_TILE // 2
    width = 3 * GROUP_W
    assert nat + width * len(dilations) == cols and seq % tm == 0
    tiles_per_seq = seq // tm
    perms = []
    for dil in dilations:
        per = tm // dil
        o = np.arange(tm)
        perm = np.zeros((tm, tm), np.float32)
        perm[o, (o % per) * dil + o // per] = 1.0
        perms.append(jnp.asarray(perm, BF16))
    const = lambda shape: pl.BlockSpec(shape, lambda i: (0,) * len(shape))
    outs = pl.pallas_call(
        functools.partial(_in_proj_prompt_kernel, dilations=tuple(dilations), n_chunks=2),
        grid=(n // tm,),
        in_specs=[pl.BlockSpec((tm, d), lambda i: (i, 0)), const((1, d)),
                  pl.BlockSpec((d, cols), lambda i: (0, 0), pipeline_mode=pl.Buffered(1))]
        + [const((tm, tm))] * len(dilations),
        out_specs=[pl.BlockSpec((tm, nat), lambda i: (i, 0))]
        + [pl.BlockSpec((None, dil, tm // dil, width), lambda i: (i // tiles_per_seq, 0, i % tiles_per_seq, 0))
           for dil in dilations],
        out_shape=[jax.ShapeDtypeStruct((n, nat), BF16)]
        + [jax.ShapeDtypeStruct((n // seq, dil, seq // dil, width), BF16) for dil in dilations],
        compiler_params=_params("parallel"),
        name="in_proj_prompt",
    )(x2d, gain, w, *perms)
    return outs[0], outs[1:]


def _attn_prompt_kernel(q_ref, kp_ref, kc_ref, vp_ref, vc_ref, o_ref, l_ref, *, slopes, dilation):
    blk = ATT_BLOCK
    j = pl.program_id(1)
    qq = lax.broadcasted_iota(jnp.int32, (blk, 2 * blk), 0)
    kk = lax.broadcasted_iota(jnp.int32, (blk, 2 * blk), 1)
    step = qq + blk - kk
    valid = (step >= 0) & (step <= blk) & ((kk >= blk) | (j > 0))
    dist = (step * dilation).astype(F32)
    low = lax.broadcasted_iota(jnp.int32, (blk, LANES), 1) < HEAD_DIM

    def residue(r, carry):
        q = q_ref[r] * (HEAD_DIM ** -0.5)
        k = jnp.concatenate([kp_ref[r], kc_ref[r]], axis=0)
        v = jnp.concatenate([vp_ref[r], vc_ref[r]], axis=0)
        rows = pl.ds(r, blk, stride=dilation) if dilation > 1 else slice(None)
        for p in range(LANE_GROUPS):
            cs = slice(p * LANES, (p + 1) * LANES)
            qg, kg, vg = q[:, cs], k[:, cs], v[:, cs]
            outs, lses = [], []
            for hh in range(2):
                sel = low if hh == 0 else jnp.logical_not(low)
                qm = jnp.where(sel, qg, jnp.zeros_like(qg))
                s = lax.dot_general(qm, kg, (((1,), (1,)), ((), ())), preferred_element_type=F32)
                s = s - slopes[2 * p + hh] * dist
                s = jnp.where(valid, s, NEG_INF)
                m = jnp.max(s, axis=-1, keepdims=True)
                e = jnp.exp(s - m)
                den = jnp.sum(e, axis=-1, keepdims=True)
                outs.append(jnp.dot(e.astype(BF16), vg, preferred_element_type=F32) / den)
                lses.append(m + jnp.log(den))
            o_ref[p, rows, :] = jnp.where(low, outs[0], outs[1])
            l_ref[p, rows, :] = jnp.where(low, lses[0], lses[1])
        return carry

    if dilation > 1:
        lax.fori_loop(0, dilation, residue, 0)
    else:
        residue(0, 0)


def _attn_prompt(zq, g, col0):
    d = ATT_DILATIONS[g]
    batch, _, n, _ = zq.shape
    nb = n // ATT_BLOCK
    blk = (None, d, ATT_BLOCK, GROUP_W)

    def cur(c):
        return pl.BlockSpec(blk, lambda b, j: (b, 0, j, c))

    def prev(c):
        return pl.BlockSpec(blk, lambda b, j: (b, 0, jnp.maximum(j - 1, 0), c))

    out_spec = pl.BlockSpec((LANE_GROUPS, ATT_BLOCK * d, LANES), lambda b, j: (0, b * nb + j, 0))
    out_sds = jax.ShapeDtypeStruct((LANE_GROUPS, batch * n * d, LANES), F32)
    kern = functools.partial(_attn_prompt_kernel, slopes=_SLOPES[g * HEADS_PER_GROUP:(g + 1) * HEADS_PER_GROUP],
                             dilation=d)
    return pl.pallas_call(
        kern,
        grid=(batch, nb),
        in_specs=[cur(col0), prev(col0 + 1), cur(col0 + 1), prev(col0 + 2), cur(col0 + 2)],
        out_specs=[out_spec, out_spec],
        out_shape=[out_sds, out_sds],
        compiler_params=_params("parallel", "arbitrary"),
        name=f"attn_prompt_g{g}",
    )(zq, zq, zq, zq, zq)


def _attn_sample_kernel(q_ref, k_ref, v_ref, c_ref, sl_ref, o_ref, l_ref, cout_ref, *, window, dilation):
    t_new = q_ref.shape[0]
    rows = HEADS_PER_GROUP * t_new
    q = q_ref[...] * (HEAD_DIM ** -0.5)
    pad = jnp.zeros((LANES - t_new, GROUP_W), F32)
    kn = jnp.concatenate([k_ref[...], pad], axis=0)
    vn = jnp.concatenate([v_ref[...], pad], axis=0)
    new_t = jnp.concatenate([kn.T, vn.T], axis=0)
    cin = c_ref[...]
    cout_ref[...] = pltpu.roll(cin, window - t_new, axis=1)
    tail = cout_ref[:, window - LANES:window]
    lane = lax.broadcasted_iota(jnp.int32, tail.shape, 1)
    cout_ref[:, window - LANES:window] = jnp.where(lane >= LANES - t_new,
                                                   pltpu.roll(new_t, LANES - t_new, axis=1), tail)
    qt = jnp.concatenate([q] * HEADS_PER_GROUP, axis=0)
    row_h = lax.broadcasted_iota(jnp.int32, (rows, GROUP_W), 0) >> _log2(t_new)
    col_h = lax.broadcasted_iota(jnp.int32, (rows, GROUP_W), 1) >> _log2(HEAD_DIM)
    qbd = jnp.where(row_h == col_h, qt, 0.0).astype(BF16)
    k_t = cin[0:GROUP_W, :].astype(BF16)
    v_t = cin[GROUP_W:2 * GROUP_W, :].astype(BF16)
    s_c = jnp.dot(qbd, k_t, preferred_element_type=F32)
    s_n = jnp.dot(qbd, new_t[0:GROUP_W, :].astype(BF16), preferred_element_type=F32)
    _log2(dilation)
    slope = sl_ref[:, 0:1]
    t_c = lax.broadcasted_iota(jnp.int32, s_c.shape, 0) & (t_new - 1)
    dist_c = window + t_c - lax.broadcasted_iota(jnp.int32, s_c.shape, 1)
    s_c = jnp.where((dist_c <= window) & ((dist_c & (dilation - 1)) == 0), s_c - slope * dist_c.astype(F32), NEG_INF)
    t_n = lax.broadcasted_iota(jnp.int32, s_n.shape, 0) & (t_new - 1)
    dist_n = t_n - lax.broadcasted_iota(jnp.int32, s_n.shape, 1)
    s_n = jnp.where((dist_n >= 0) & ((dist_n & (dilation - 1)) == 0), s_n - slope * dist_n.astype(F32), NEG_INF)
    m = jnp.maximum(jnp.max(s_c, axis=-1, keepdims=True), jnp.max(s_n, axis=-1, keepdims=True))
    e_c = jnp.exp(s_c - m)
    e_n = jnp.exp(s_n - m)
    den = jnp.sum(e_c, axis=-1, keepdims=True) + jnp.sum(e_n, axis=-1, keepdims=True)
    o_full = lax.dot_general(e_c.astype(BF16), v_t, (((1,), (1,)), ((), ())), preferred_element_type=F32)
    o_full = (o_full + jnp.dot(e_n.astype(BF16), vn.astype(BF16), preferred_element_type=F32)) / den
    lse = m + jnp.log(den)
    out_h = lax.broadcasted_iota(jnp.int32, (t_new, GROUP_W), 1) >> _log2(HEAD_DIM)
    o = jnp.zeros((t_new, GROUP_W), F32)
    l = jnp.zeros((t_new, GROUP_W), F32)
    for h in range(HEADS_PER_GROUP):
        rs = slice(h * t_new, (h + 1) * t_new)
        o = jnp.where(out_h == h, o_full[rs, :], o)
        l = jnp.where(out_h == h, lse[rs, :], l)
    for c in range(LANE_GROUPS):
        o_ref[c] = o[:, c * LANES:(c + 1) * LANES]
        l_ref[c] = l[:, c * LANES:(c + 1) * LANES]


def _attn_sample(z, cache, g):
    batch, window = cache.shape[0], cache.shape[1]
    n_tok = z.shape[0]
    t_new = n_tok // batch
    d = ATT_DILATIONS[g]
    c_t = jnp.transpose(cache, (0, 2, 3, 4, 1)).reshape(batch, 2 * GROUP_W, window)
    slopes = np.repeat(np.asarray(_SLOPES[g * HEADS_PER_GROUP:(g + 1) * HEADS_PER_GROUP], np.float32), t_new)
    slopes = jnp.asarray(np.broadcast_to(slopes[:, None], (HEADS_PER_GROUP * t_new, LANES)))
    col0 = COL_QKV + 3 * g

    def col(c):
        return pl.BlockSpec((t_new, GROUP_W), lambda b: (b, c))

    tok_spec = pl.BlockSpec((LANE_GROUPS, t_new, LANES), lambda b: (0, b, 0))
    tok_sds = jax.ShapeDtypeStruct((LANE_GROUPS, n_tok, LANES), F32)
    buf_spec = pl.BlockSpec((None, 2 * GROUP_W, window), lambda b: (b, 0, 0))
    o, l, cout = pl.pallas_call(
        functools.partial(_attn_sample_kernel, window=window, dilation=d),
        grid=(batch,),
        in_specs=[col(col0), col(col0 + 1), col(col0 + 2), buf_spec, pl.BlockSpec(slopes.shape, lambda b: (0, 0))],
        out_specs=[tok_spec, tok_spec, buf_spec],
        out_shape=[tok_sds, tok_sds, jax.ShapeDtypeStruct(c_t.shape, F32)],
        compiler_params=_params("parallel"),
        name=f"attn_sample_g{g}",
    )(z, z, z, c_t, slopes)
    cout = cout.reshape(batch, 2, HEADS_PER_GROUP, HEAD_DIM, window)
    return o, l, jnp.transpose(cout, (0, 4, 1, 2, 3))


def _gelu_tanh(x):
    return x * (0.5 * (1.0 + jnp.tanh(np.sqrt(2.0 / np.pi).astype(np.float32) * (x + 0.044715 * (x * x * x)))))


def _rglru_kernel(xb_ref, yg_ref, c0_ref, h0_ref, cw_ref, cb_ref, wa_ref, wi_ref, ba_ref, bi_ref, lam_ref,
                  yb_ref, cout_ref, hout_ref, xpad, a_s, u_s, h_s, *, tc, seg, reset_first):
    nb = xb_ref.shape[0]
    ngl = a_s.shape[0]
    j = pl.program_id(1)

    @pl.when(j == 0)
    def _():
        xpad[:, 0:SUBLANES, :] = c0_ref[...]
        h_s[...] = h0_ref[...]

    @pl.when(j > 0)
    def _():
        xpad[:, 0:SUBLANES, :] = xpad[:, tc:tc + SUBLANES, :]

    xpad[:, SUBLANES:SUBLANES + tc, :] = xb_ref[...].astype(F32)
    cout_ref[...] = xpad[:, tc:tc + SUBLANES, :]

    lam = lam_ref[...]
    softplus_neg = jnp.maximum(-lam, 0.0) + jnp.log1p(jnp.exp(-jnp.abs(lam)))
    first = (lax.broadcasted_iota(jnp.int32, (tc, 1), 0) == 0) & (j == 0)
    gw = wa_ref.shape[1]
    for b in range(nb):
        xc = cb_ref[...]
        for tap in range(CONV_WIDTH):
            off = SUBLANES - (CONV_WIDTH - 1) + tap
            xc = xc + xpad[b, off:off + tc, :] * cw_ref[tap:tap + 1, :]
        xcb = xc.astype(BF16)
        ra, ri = [], []
        for blk in range(wa_ref.shape[0]):
            xs = xcb[:, blk * gw:(blk + 1) * gw]
            ra.append(jnp.dot(xs, wa_ref[blk], preferred_element_type=F32))
            ri.append(jnp.dot(xs, wi_ref[blk], preferred_element_type=F32))
        r = _sigmoid(jnp.concatenate(ra, axis=-1) + ba_ref[...])
        i = _sigmoid(jnp.concatenate(ri, axis=-1) + bi_ref[...])
        log_a = -LRU_C * r * softplus_neg
        a = jnp.exp(log_a)
        mult = jnp.sqrt(1.0 - a * a)
        if reset_first:
            mult = jnp.where(first, 1.0, mult)
        u = mult * (i * xc)
        for c in range(ngl):
            a_s[c, b * seg:b * seg + tc, :] = a[:, c * LANES:(c + 1) * LANES]
            u_s[c, b * seg:b * seg + tc, :] = u[:, c * LANES:(c + 1) * LANES]

    def step(t, hs):
        idx = pl.ds(t, nb, stride=seg)
        out = []
        for c in range(ngl):
            h = a_s[c, idx, :] * hs[c] + u_s[c, idx, :]
            u_s[c, idx, :] = h
            out.append(h)
        return tuple(out)

    hs = lax.fori_loop(0, tc, step, tuple(h_s[:, c * LANES:(c + 1) * LANES] for c in range(ngl)))
    h = jnp.concatenate(hs, axis=-1)
    h_s[...] = h
    hout_ref[...] = h
    for b in range(nb):
        hb = jnp.concatenate([u_s[c, b * seg:b * seg + tc, :] for c in range(ngl)], axis=-1)
        yb_ref[b] = (_gelu_tanh(yg_ref[b].astype(F32)) * hb).astype(yb_ref.dtype)


def _rglru(z3, conv0, h0, p, reset_first):
    batch, t, _ = z3.shape
    width = h0.shape[-1]
    nb = SUBLANES
    tc = min(128, t)
    seg = tc + SUBLANES
    c0 = jnp.pad(conv0, ((0, 0), (SUBLANES - (CONV_WIDTH - 1), 0), (0, 0)))
    cw = jnp.pad(p["conv_w"], ((0, SUBLANES - CONV_WIDTH), (0, 0)))
    ngrp = p["wa"].shape[0]
    gw = p["wa"].shape[1]

    def full(shape):
        return pl.BlockSpec(shape, lambda i, j: (0,) * len(shape))

    yb, cout, hout = pl.pallas_call(
        functools.partial(_rglru_kernel, tc=tc, seg=seg, reset_first=reset_first),
        grid=(batch // nb, t // tc),
        in_specs=[
            pl.BlockSpec((nb, tc, width), lambda i, j: (i, j, 0)),
            pl.BlockSpec((nb, tc, width), lambda i, j: (i, j, 1)),
            pl.BlockSpec((nb, SUBLANES, width), lambda i, j: (i, 0, 0)),
            pl.BlockSpec((nb, width), lambda i, j: (i, 0)),
            full((SUBLANES, width)), full((1, width)),
            full((ngrp, gw, gw)), full((ngrp, gw, gw)),
            full((1, width)), full((1, width)), full((1, width)),
        ],
        out_specs=[
            pl.BlockSpec((nb, tc, width), lambda i, j: (i, j, 0)),
            pl.BlockSpec((nb, SUBLANES, width), lambda i, j: (i, 0, 0)),
            pl.BlockSpec((nb, width), lambda i, j: (i, 0)),
        ],
        out_shape=[
            jax.ShapeDtypeStruct((batch, t, width), BF16),
            jax.ShapeDtypeStruct((batch, SUBLANES, width), F32),
            jax.ShapeDtypeStruct((batch, width), F32),
        ],
        scratch_shapes=[
            pltpu.VMEM((nb, tc + SUBLANES, width), F32),
            pltpu.VMEM((width // LANES, nb * seg, LANES), F32),
            pltpu.VMEM((width // LANES, nb * seg, LANES), F32),
            pltpu.VMEM((nb, width), F32),
        ],
        compiler_params=_params("parallel", "arbitrary"),
        name="rglru",
    )(z3, z3, c0, h0, cw, p["conv_b"], p["wa"], p["wi"], p["ba"], p["bi"], p["lam"])
    return yb, cout[:, SUBLANES - (CONV_WIDTH - 1):, :], hout


def _mix_kernel(o1, o2, o3, l1, l2, l3, yb_ref, ga_ref, gb_ref, x_ref, wpa_ref, wpb_ref, wo_ref, nf_ref,
                wr_ref, br_ref, x1_ref, xn_ref, idx_ref, gate_ref):
    parts = []
    for c in range(LANE_GROUPS):
        la, lb, lc = l1[c], l2[c], l3[c]
        m = jnp.maximum(jnp.maximum(la, lb), lc)
        ea, eb, ec = jnp.exp(la - m), jnp.exp(lb - m), jnp.exp(lc - m)
        inv = 1.0 / (ea + eb + ec)
        parts.append((ea * inv) * o1[c] + (eb * inv) * o2[c] + (ec * inv) * o3[c])
    att = jnp.concatenate(parts, axis=-1)
    y_a = jnp.dot(att.astype(BF16), wpa_ref[...], preferred_element_type=F32)
    y_b = jnp.dot(yb_ref[...], wpb_ref[...], preferred_element_type=F32)
    merged = _sigmoid(ga_ref[...].astype(F32)) * y_a + _sigmoid(gb_ref[...].astype(F32)) * y_b
    x1 = x_ref[...] + jnp.dot(merged.astype(BF16), wo_ref[...], preferred_element_type=F32)
    x1_ref[...] = x1
    ms = jnp.mean(x1 * x1, axis=-1, keepdims=True)
    xn = (x1 * lax.rsqrt(ms + RMS_EPS) * nf_ref[...]).astype(BF16)
    xn_ref[...] = xn
    logits = jnp.dot(xn, wr_ref[...], preferred_element_type=F32) + br_ref[...]
    lane = lax.broadcasted_iota(jnp.int32, logits.shape, 1)
    lane_f = lane.astype(F32)
    logits = jnp.where(lane < N_EXPERTS, logits, NEG_INF)
    vals, idxs = [], []
    for _ in range(TOP_K):
        mk = jnp.max(logits, axis=-1, keepdims=True)
        ik = jnp.min(jnp.where(logits == mk, lane_f, float(LANES)), axis=-1, keepdims=True)
        logits = jnp.where(lane_f == ik, NEG_INF, logits)
        vals.append(mk)
        idxs.append(ik)
    es = [jnp.exp(vk - vals[0]) for vk in vals]
    tot = es[0] + es[1] + es[2] + es[3]
    idx_out = jnp.zeros(logits.shape, F32)
    gate_out = jnp.zeros(logits.shape, F32)
    for kk in range(TOP_K):
        idx_out = jnp.where(lane == kk, idxs[kk], idx_out)
        gate_out = jnp.where(lane == kk, es[kk] / tot, gate_out)
    idx_ref[...] = idx_out.astype(jnp.int32)
    gate_ref[...] = gate_out


def _mix(os_, ls_, yb, z, x2d, p):
    n, d = x2d.shape
    tm = min(256, n)

    def rows(w, c=0):
        return pl.BlockSpec((tm, w), lambda i: (i, c))

    def full(a):
        return pl.BlockSpec(a.shape, lambda i: (0,) * a.ndim)

    grp = pl.BlockSpec((LANE_GROUPS, tm, LANES), lambda i: (0, i, 0))
    x1, xn, idx, gate = pl.pallas_call(
        _mix_kernel,
        grid=(n // tm,),
        in_specs=[grp] * 6 + [rows(d), rows(d, 2), rows(d, 3), rows(d),
                              full(p["w_pa"]), full(p["w_pb"]), full(p["w_o"]), full(p["norm_ffn"]),
                              full(p["w_router"]), full(p["b_router"])],
        out_specs=[rows(d), rows(d), rows(LANES), rows(LANES)],
        out_shape=[jax.ShapeDtypeStruct((n, d), F32), jax.ShapeDtypeStruct((n, d), BF16),
                   jax.ShapeDtypeStruct((n, LANES), jnp.int32), jax.ShapeDtypeStruct((n, LANES), F32)],
        compiler_params=_params("parallel"),
        name="mix",
    )(*os_, *ls_, yb, z, z, x2d, p["w_pa"], p["w_pb"], p["w_o"], p["norm_ffn"], p["w_router"], p["b_router"])
    return x1, xn, idx, gate


def _expert_kernel(be_ref, nused_ref, x_ref, w1_ref, b1_ref, w2_ref, b2_ref, perm_ref, y_ref, w1_s, w2_s):
    i = pl.program_id(0)
    dff = w2_ref.shape[0]
    n_cb = w1_ref.shape[1] // MXU_DIM
    half = MXU_DIM // 2

    @pl.when((i == 0) | (be_ref[i] != be_ref[jnp.maximum(i - 1, 0)]))
    def _():
        for cb in range(n_cb):
            cs = slice(cb * MXU_DIM, (cb + 1) * MXU_DIM)
            w1_s[:, cs] = jnp.dot(w1_ref[:, cs].astype(BF16), perm_ref[...], preferred_element_type=F32).astype(BF16)
        w2_s[...] = w2_ref[...].astype(BF16)

    @pl.when(i < nused_ref[0])
    def _():
        h = jnp.dot(x_ref[...], w1_s[...], preferred_element_type=F32) + b1_ref[...]
        acts = []
        for cb in range(n_cb):
            x_glu = jnp.minimum(h[:, cb * MXU_DIM:cb * MXU_DIM + half], SWIGLU_LIMIT)
            x_lin = jnp.clip(h[:, cb * MXU_DIM + half:(cb + 1) * MXU_DIM], -SWIGLU_LIMIT, SWIGLU_LIMIT)
            acts.append(x_glu * _sigmoid(SWIGLU_ALPHA * x_glu) * (x_lin + 1.0))
        act = jnp.concatenate(acts, axis=-1)
        assert act.shape[-1] == dff
        y = jnp.dot(act.astype(BF16), w2_s[...], preferred_element_type=F32) + b2_ref[...]
        y_ref[...] = y.astype(y_ref.dtype)

    @pl.when(i >= nused_ref[0])
    def _():
        y_ref[...] = jnp.zeros_like(y_ref)


def _experts(x_rows, block_expert, n_used, p):
    n_rows, d = x_rows.shape
    n_blocks = n_rows // MOE_BLOCK
    dff2 = p["w1"].shape[-1]
    half = MXU_DIM // 2
    o = np.arange(MXU_DIM)
    perm = np.zeros((MXU_DIM, MXU_DIM), np.float32)
    perm[np.where(o < half, 2 * o, 2 * (o - half) + 1), o] = 1.0
    grid_spec = pltpu.PrefetchScalarGridSpec(
        num_scalar_prefetch=2,
        grid=(n_blocks,),
        in_specs=[
            pl.BlockSpec((MOE_BLOCK, d), lambda i, be, nu: (i, 0)),
            pl.BlockSpec((None, d, dff2), lambda i, be, nu: (be[i], 0, 0)),
            pl.BlockSpec((None, 1, dff2), lambda i, be, nu: (be[i], 0, 0)),
            pl.BlockSpec((None, dff2 // 2, d), lambda i, be, nu: (be[i], 0, 0)),
            pl.BlockSpec((None, 1, d), lambda i, be, nu: (be[i], 0, 0)),
            pl.BlockSpec((MXU_DIM, MXU_DIM), lambda i, be, nu: (0, 0)),
        ],
        out_specs=pl.BlockSpec((MOE_BLOCK, d), lambda i, be, nu: (i, 0)),
        scratch_shapes=[pltpu.VMEM((d, dff2), BF16), pltpu.VMEM((dff2 // 2, d), BF16)],
    )
    return pl.pallas_call(
        _expert_kernel,
        grid_spec=grid_spec,
        out_shape=jax.ShapeDtypeStruct((n_rows, d), BF16),
        compiler_params=_params("arbitrary"),
        name="experts",
    )(block_expert, n_used, x_rows, p["w1"], p["b1"], p["w2"], p["b2"], jnp.asarray(perm, BF16))


def _combine_kernel(y0, y1, y2, y3, gate_ref, x1_ref, nf_ref, o_ref):
    g = gate_ref[...]
    y = jnp.zeros(x1_ref.shape, F32)
    for kk, y_ref in enumerate((y0, y1, y2, y3)):
        y = y + y_ref[...].astype(F32) * g[:, kk:kk + 1]
    x = x1_ref[...] + y
    ms = jnp.mean(x * x, axis=-1, keepdims=True)
    o_ref[...] = x * lax.rsqrt(ms + RMS_EPS) * nf_ref[...]


def _combine(ys, gate, x1, norm_final):
    n, d = x1.shape
    tm = min(256, n)
    row = pl.BlockSpec((tm, d), lambda i: (i, 0))
    return pl.pallas_call(
        _combine_kernel,
        grid=(n // tm,),
        in_specs=[row] * TOP_K + [pl.BlockSpec((tm, LANES), lambda i: (i, 0)), row,
                                  pl.BlockSpec((1, d), lambda i: (0, 0))],
        out_specs=row,
        out_shape=jax.ShapeDtypeStruct((n, d), F32),
        compiler_params=_params("parallel"),
        name="combine",
    )(*ys, gate, x1, norm_final)


def _rank_kernel(idx_ref, tri_ref, rank_ref, cnt_ref, carry):
    @pl.when(pl.program_id(0) == 0)
    def _():
        carry[...] = jnp.zeros_like(carry)

    idx = idx_ref[...]
    lane = lax.broadcasted_iota(jnp.int32, idx.shape, 1)
    hits = [lane == idx[:, kk:kk + 1] for kk in range(TOP_K)]
    onehot = jnp.zeros(idx.shape, F32)
    for hit in hits:
        onehot = onehot + jnp.where(hit, 1.0, 0.0)
    before = jnp.dot(tri_ref[...], onehot.astype(BF16), preferred_element_type=F32) + carry[0:1, :]
    out = jnp.zeros(idx.shape, F32)
    for kk, hit in enumerate(hits):
        out = jnp.where(lane == kk, jnp.sum(jnp.where(hit, before, 0.0), axis=-1, keepdims=True), out)
    rank_ref[...] = out.astype(jnp.int32)
    carry[...] = carry[...] + jnp.sum(onehot, axis=0, keepdims=True)
    cnt_ref[...] = carry[...].astype(jnp.int32)


def _rank(idx):
    n = idx.shape[0]
    tm = 512
    while n % tm:
        tm //= 2
    tri = jnp.asarray(np.tril(np.ones((tm, tm), np.float32), -1), BF16)
    return pl.pallas_call(
        _rank_kernel,
        grid=(n // tm,),
        in_specs=[pl.BlockSpec((tm, LANES), lambda i: (i, 0)), pl.BlockSpec((tm, tm), lambda i: (0, 0))],
        out_specs=[pl.BlockSpec((tm, LANES), lambda i: (i, 0)), pl.BlockSpec((SUBLANES, LANES), lambda i: (0, 0))],
        out_shape=[jax.ShapeDtypeStruct((n, LANES), jnp.int32), jax.ShapeDtypeStruct((SUBLANES, LANES), jnp.int32)],
        scratch_shapes=[pltpu.VMEM((SUBLANES, LANES), F32)],
        compiler_params=_params("arbitrary"),
        name="rank",
    )(idx, tri)


def _route(idx):
    n_tok = idx.shape[0]
    n_assign = n_tok * TOP_K
    rank, cnt = _rank(idx)
    counts = cnt[0, :N_EXPERTS]
    top_idx = idx[:, :TOP_K]
    e_flat = top_idx.reshape(-1)
    order = jnp.argsort(e_flat)
    padded = (counts + MOE_BLOCK - 1) // MOE_BLOCK * MOE_BLOCK
    starts = jnp.cumsum(counts) - counts
    pends = jnp.cumsum(padded)
    pstarts = pends - padded
    dest = (pstarts[top_idx] + rank[:, :TOP_K]).reshape(-1)
    n_blocks = -(-(n_assign + N_EXPERTS * (MOE_BLOCK - 1)) // MOE_BLOCK)
    n_rows = n_blocks * MOE_BLOCK
    block_start = jnp.arange(n_blocks, dtype=jnp.int32) * MOE_BLOCK
    block_expert = jnp.minimum(jnp.sum((pends[None, :] <= block_start[:, None]).astype(jnp.int32), axis=1),
                               N_EXPERTS - 1)
    row_e = jnp.repeat(block_expert, MOE_BLOCK)
    local = jnp.arange(n_rows, dtype=jnp.int32) - pstarts[row_e]
    live = (local >= 0) & (local < counts[row_e])
    src = order[jnp.clip(starts[row_e] + local, 0, n_assign - 1)] // TOP_K
    row_tok = jnp.where(live, src, n_tok).astype(jnp.int32)
    n_used = (pends[-1] // MOE_BLOCK).astype(jnp.int32).reshape(1)
    return row_tok, dest.reshape(n_tok, TOP_K), block_expert, n_used


def _mixer(x, p, caches, conv0, h0):
    batch, t, d = x.shape
    prompt = caches is None
    x2d = x.reshape(batch * t, d)
    os_, ls_, new_kv = [], [], []
    if prompt:
        nat = N_NATURAL_BLOCKS * GROUP_W
        z, zd = _in_proj_prompt(x2d, p["norm_mix"], p["w_in"], t, nat, ATT_DILATIONS[1:])
        zq = [z.reshape(batch, 1, t, nat)] + list(zd)
        for g in range(N_GROUPS):
            dil = ATT_DILATIONS[g]
            o, l = _attn_prompt(zq[g], g, COL_QKV if g == 0 else 0)
            keep = min(ATT_WINDOWS[g], t)
            if g == 0:
                kcol = (COL_QKV + 1) * GROUP_W
                last = z.reshape(batch, t, nat)[:, t - keep:, kcol:kcol + 2 * GROUP_W]
            else:
                last = zq[g][:, :, (t - keep) // dil:, GROUP_W:3 * GROUP_W]
                last = jnp.swapaxes(last, 1, 2).reshape(batch, keep, 2 * GROUP_W)
            kv = last.astype(F32).reshape(batch, keep, 2, HEADS_PER_GROUP, HEAD_DIM)
            os_.append(o)
            ls_.append(l)
            new_kv.append(kv[None])
        conv0 = jnp.zeros((batch, CONV_WIDTH - 1, d), F32)
        h0 = jnp.zeros((batch, d), F32)
    else:
        z = _in_proj(x2d, p["norm_mix"], p["w_in"], F32, 4)
        for g in range(N_GROUPS):
            o, l, kv = _attn_sample(z, caches[g], g)
            os_.append(o)
            ls_.append(l)
            new_kv.append(kv[None])
    yb, new_conv, h_last = _rglru(z.reshape(batch, t, -1), conv0, h0, p, prompt)
    x1, xn, idx, gate = _mix(os_, ls_, yb.reshape(batch * t, d), z, x2d, p)
    return x1, xn, idx, gate, new_kv, new_conv[None], h_last[None]


def kernel(x_prompt, x_sample, cache_kv_g1, cache_kv_g2, cache_kv_g3, state_conv, state_h, norm_mix, w_in, w_pa,
           w_pb, w_o, conv_w, conv_b, lru_wa, lru_ba, lru_wi, lru_bi, lru_lambda, norm_ffn, w_router, b_router,
           w1, b1, w2, b2, norm_final):
    d = x_prompt.shape[-1]
    gsz = N_GROUPS * GROUP_W

    def block_diag(w):
        per = MXU_DIM // LRU_BLOCK_W
        w = w.reshape(-1, per, LRU_BLOCK_W, LRU_BLOCK_W)
        eye = jnp.eye(per, dtype=w.dtype)
        return jnp.einsum("gacd,ab->gacbd", w, eye).reshape(-1, MXU_DIM, MXU_DIM).astype(BF16)

    w_in0 = w_in[0]
    qkv = [w_in0[:, which * gsz + g * GROUP_W: which * gsz + (g + 1) * GROUP_W]
           for g in range(N_GROUPS) for which in range(3)]
    n_e, dff2 = b1.shape[1], b1.shape[2]
    half = MXU_DIM // 2
    p = dict(
        norm_mix=norm_mix[0][None], norm_ffn=norm_ffn[0][None],
        w_in=jnp.concatenate([w_in0[:, 3 * gsz:]] + qkv, axis=1).astype(BF16),
        w_pa=w_pa[0].astype(BF16), w_pb=w_pb[0].astype(BF16), w_o=w_o[0].astype(BF16),
        conv_w=conv_w[0], conv_b=conv_b[0][None],
        wa=block_diag(lru_wa[0]), wi=block_diag(lru_wi[0]),
        ba=lru_ba[0].reshape(1, d), bi=lru_bi[0].reshape(1, d), lam=lru_lambda[0][None],
        w_router=jnp.pad(w_router[0], ((0, 0), (0, LANES - N_EXPERTS))).astype(BF16),
        b_router=jnp.pad(b_router[0], (0, LANES - N_EXPERTS))[None],
        w1=w1[0], w2=w2[0],
        b1=b1[0].reshape(n_e, dff2 // MXU_DIM, half, 2).transpose(0, 1, 3, 2).reshape(n_e, 1, dff2),
        b2=b2[0][:, None, :],
    )

    x1_p, xn_p, idx_p, gate_p, kv_p, conv_p, h_p = _mixer(x_prompt, p, None, None, None)
    x1_s, xn_s, idx_s, gate_s, kv_s, conv_s, h_s = _mixer(
        x_sample, p, [cache_kv_g1[0], cache_kv_g2[0], cache_kv_g3[0]], state_conv[0], state_h[0])

    n_p = x1_p.shape[0]
    row_tok, dest, block_expert, n_used = _route(jnp.concatenate([idx_p, idx_s], axis=0))
    xn_all = jnp.concatenate([xn_p, xn_s, jnp.zeros((1, d), BF16)], axis=0)
    y_rows = _experts(xn_all[row_tok], block_expert, n_used, p)
    nf = norm_final[None]
    y_p = _combine([y_rows[dest[:n_p, kk]] for kk in range(TOP_K)], gate_p, x1_p, nf).reshape(x_prompt.shape)
    y_s = _combine([y_rows[dest[n_p:, kk]] for kk in range(TOP_K)], gate_s, x1_s, nf).reshape(x_sample.shape)
    return (y_p, y_s, kv_p[0], kv_p[1], kv_p[2], conv_p, h_p, kv_s[0], kv_s[1], kv_s[2], conv_s, h_s)
```

```python
import functools

import numpy as np
import jax
import jax.numpy as jnp
from jax import lax
from jax.experimental import pallas as pl
from jax.experimental.pallas import tpu as pltpu

F32 = jnp.float32
BF16 = jnp.bfloat16

N_GROUPS = 3
HEADS_PER_GROUP = 8
HEAD_DIM = 64
GROUP_W = HEADS_PER_GROUP * HEAD_DIM
ATT_WINDOWS = (128, 512, 2048)
ATT_DILATIONS = (1, 4, 16)
ATT_BLOCK = 128
N_ATT_HEADS = N_GROUPS * HEADS_PER_GROUP
CONV_WIDTH = 4
LRU_C = 8.0
LRU_BLOCK_W = 64
N_EXPERTS = 32
TOP_K = 4
SWIGLU_ALPHA = 1.702
SWIGLU_LIMIT = 7.0
RMS_EPS = 1e-6
NEG_INF = float("-inf")

LANES = 128
SUBLANES = 8
MXU_DIM = 256
VMEM_LIMIT_BYTES = 56 * 1024 * 1024

COL_QKV = 8
N_NATURAL_BLOCKS = COL_QKV + 3
LANE_GROUPS = GROUP_W // LANES
MOE_BLOCK = 512
PROJ_TILE = 512

_SLOPES = [float(np.float32(2.0 ** (-8.0 * (i + 1) / N_ATT_HEADS))) for i in range(N_ATT_HEADS)]


def _params(*sem):
    return pltpu.CompilerParams(dimension_semantics=sem, vmem_limit_bytes=VMEM_LIMIT_BYTES)


def _sigmoid(x):
    return 0.5 * jnp.tanh(0.5 * x) + 0.5


def _log2(n):
    assert n > 0 and n & (n - 1) == 0, n
    return n.bit_length() - 1


def _rmsnorm_bf16(x, gain):
    ms = jnp.mean(x * x, axis=-1, keepdims=True)
    return (x * lax.rsqrt(ms + RMS_EPS) * gain).astype(BF16)


def _in_proj_kernel(x_ref, g_ref, w_ref, o_ref, xn_ref):
    @pl.when(pl.program_id(1) == 0)
    def _():
        xn_ref[...] = _rmsnorm_bf16(x_ref[...], g_ref[...])

    o_ref[...] = jnp.dot(xn_ref[...], w_ref[...], preferred_element_type=F32).astype(o_ref.dtype)


def _in_proj(x2d, gain, w, out_dtype, n_col_tiles):
    n, d = x2d.shape
    cols = w.shape[1]
    tm = min(PROJ_TILE, n)
    tn = cols // n_col_tiles
    return pl.pallas_call(
        _in_proj_kernel,
        grid=(n // tm, n_col_tiles),
        in_specs=[
            pl.BlockSpec((tm, d), lambda i, j: (i, 0)),
            pl.BlockSpec((1, d), lambda i, j: (0, 0)),
            pl.BlockSpec((d, tn), lambda i, j: (0, j)),
        ],
        out_specs=pl.BlockSpec((tm, tn), lambda i, j: (i, j)),
        out_shape=jax.ShapeDtypeStruct((n, cols), out_dtype),
        scratch_shapes=[pltpu.VMEM((tm, d), BF16)],
        compiler_params=_params("parallel", "arbitrary"),
        name="in_proj",
    )(x2d, gain, w)


def _in_proj_prompt_kernel(x_ref, g_ref, w_ref, *refs, dilations, n_chunks):
    n_dil = len(dilations)
    perm_refs, zn_ref, zd_refs = refs[:n_dil], refs[n_dil], refs[n_dil + 1:]
    xn = _rmsnorm_bf16(x_ref[...], g_ref[...])
    nat = zn_ref.shape[1]
    cw = nat // n_chunks
    for c in range(n_chunks):
        cs = slice(c * cw, (c + 1) * cw)
        zn_ref[:, cs] = jnp.dot(xn, w_ref[:, cs], preferred_element_type=F32).astype(zn_ref.dtype)
    col = nat
    for perm_ref, zd_ref, dil in zip(perm_refs, zd_refs, dilations):
        width = zd_ref.shape[-1]
        xp = jnp.dot(perm_ref[...], xn, preferred_element_type=F32).astype(BF16)
        res = jnp.dot(xp, w_ref[:, col:col + width], preferred_element_type=F32).astype(zd_ref.dtype)
        per = res.shape[0] // dil
        for r in range(dil):
            zd_ref[r] = res[r * per:(r + 1) * per, :]
        col += width


def _in_proj_prompt(x2d, gain, w, seq, nat, dilations):
    n, d = x2d.shape
    cols = w.shape[1]
    tm = PROJ_TILE // 2
    width = 3 * GROUP_W
    assert nat + width * len(dilations) == cols and seq % tm == 0
    tiles_per_seq = seq // tm
    perms = []
    for dil in dilations:
        per = tm // dil
        o = np.arange(tm)
        perm = np.zeros((tm, tm), np.float32)
        perm[o, (o % per) * dil + o // per] = 1.0
        perms.append(jnp.asarray(perm, BF16))
    const = lambda shape: pl.BlockSpec(shape, lambda i: (0,) * len(shape))
    outs = pl.pallas_call(
        functools.partial(_in_proj_prompt_kernel, dilations=tuple(dilations), n_chunks=2),
        grid=(n // tm,),
        in_specs=[pl.BlockSpec((tm, d), lambda i: (i, 0)), const((1, d)),
                  pl.BlockSpec((d, cols), lambda i: (0, 0), pipeline_mode=pl.Buffered(1))]
        + [const((tm, tm))] * len(dilations),
        out_specs=[pl.BlockSpec((tm, nat), lambda i: (i, 0))]
        + [pl.BlockSpec((None, dil, tm // dil, width), lambda i: (i // tiles_per_seq, 0, i % tiles_per_seq, 0))
           for dil in dilations],
        out_shape=[jax.ShapeDtypeStruct((n, nat), BF16)]
        + [jax.ShapeDtypeStruct((n // seq, dil, seq // dil, width), BF16) for dil in dilations],
        compiler_params=_params("parallel"),
        name="in_proj_prompt",
    )(x2d, gain, w, *perms)
    return outs[0], outs[1:]


def _attn_prompt_kernel(q_ref, kp_ref, kc_ref, vp_ref, vc_ref, b_ref, o_ref, l_ref, *, dilation):
    blk = ATT_BLOCK
    low = lax.broadcasted_iota(jnp.int32, (blk, LANES), 1) < HEAD_DIM

    def residue(r, carry):
        q = q_ref[r] * (HEAD_DIM ** -0.5)
        k = jnp.concatenate([kp_ref[r], kc_ref[r]], axis=0)
        v = jnp.concatenate([vp_ref[r], vc_ref[r]], axis=0)
        rows = pl.ds(r, blk, stride=dilation) if dilation > 1 else slice(None)
        for p in range(LANE_GROUPS):
            cs = slice(p * LANES, (p + 1) * LANES)
            qg, kg, vg = q[:, cs], k[:, cs], v[:, cs]
            outs, lses = [], []
            for hh in range(2):
                sel = low if hh == 0 else jnp.logical_not(low)
                qm = jnp.where(sel, qg, jnp.zeros_like(qg))
                s = lax.dot_general(qm, kg, (((1,), (1,)), ((), ())), preferred_element_type=F32)
                s = s + b_ref[2 * p + hh]
                m = jnp.max(s, axis=-1, keepdims=True)
                e = jnp.exp(s - m)
                den = jnp.sum(e, axis=-1, keepdims=True)
                outs.append(jnp.dot(e.astype(BF16), vg, preferred_element_type=F32) / den)
                lses.append(m + jnp.log(den))
            o_ref[p, rows, :] = jnp.where(low, outs[0], outs[1])
            l_ref[p, rows, :] = jnp.where(low, lses[0], lses[1])
        return carry

    if dilation > 1:
        lax.fori_loop(0, dilation, residue, 0)
    else:
        residue(0, 0)


def _band_bias(g):
    blk, d = ATT_BLOCK, ATT_DILATIONS[g]
    qq = np.arange(blk)[:, None]
    kk = np.arange(2 * blk)[None, :]
    step = qq + blk - kk
    band = (step >= 0) & (step <= blk)
    slopes = np.asarray(_SLOPES[g * HEADS_PER_GROUP:(g + 1) * HEADS_PER_GROUP], np.float32)
    bias = -slopes[:, None, None] * (step * d).astype(np.float32)[None]
    out = np.empty((2, HEADS_PER_GROUP, blk, 2 * blk), np.float32)
    out[0] = np.where(band & (kk >= blk), bias, -np.inf)
    out[1] = np.where(band, bias, -np.inf)
    return jnp.asarray(out)


def _attn_prompt(zq, g, col0):
    d = ATT_DILATIONS[g]
    batch, _, n, _ = zq.shape
    nb = n // ATT_BLOCK
    blk = (None, d, ATT_BLOCK, GROUP_W)

    def cur(c):
        return pl.BlockSpec(blk, lambda b, j: (b, 0, j, c))

    def prev(c):
        return pl.BlockSpec(blk, lambda b, j: (b, 0, jnp.maximum(j - 1, 0), c))

    out_spec = pl.BlockSpec((LANE_GROUPS, ATT_BLOCK * d, LANES), lambda b, j: (0, b * nb + j, 0))
    out_sds = jax.ShapeDtypeStruct((LANE_GROUPS, batch * n * d, LANES), F32)
    bias = _band_bias(g)
    bias_spec = pl.BlockSpec((None,) + bias.shape[1:], lambda b, j: (jnp.minimum(j, 1), 0, 0, 0))
    return pl.pallas_call(
        functools.partial(_attn_prompt_kernel, dilation=d),
        grid=(batch, nb),
        in_specs=[cur(col0), prev(col0 + 1), cur(col0 + 1), prev(col0 + 2), cur(col0 + 2), bias_spec],
        out_specs=[out_spec, out_spec],
        out_shape=[out_sds, out_sds],
        compiler_params=_params("parallel", "arbitrary"),
        name=f"attn_prompt_g{g}",
    )(zq, zq, zq, zq, zq, bias)


def _attn_sample_kernel(q_ref, k_ref, v_ref, c_ref, sl_ref, o_ref, l_ref, cout_ref, *, window, dilation):
    t_new = q_ref.shape[0]
    rows = HEADS_PER_GROUP * t_new
    q = q_ref[...] * (HEAD_DIM ** -0.5)
    pad = jnp.zeros((LANES - t_new, GROUP_W), F32)
    kn = jnp.concatenate([k_ref[...], pad], axis=0)
    vn = jnp.concatenate([v_ref[...], pad], axis=0)
    new_t = jnp.concatenate([kn.T, vn.T], axis=0)
    cin = c_ref[...]
    cout_ref[...] = pltpu.roll(cin, window - t_new, axis=1)
    tail = cout_ref[:, window - LANES:window]
    lane = lax.broadcasted_iota(jnp.int32, tail.shape, 1)
    cout_ref[:, window - LANES:window] = jnp.where(lane >= LANES - t_new,
                                                   pltpu.roll(new_t, LANES - t_new, axis=1), tail)
    qt = jnp.concatenate([q] * HEADS_PER_GROUP, axis=0)
    row_h = lax.broadcasted_iota(jnp.int32, (rows, GROUP_W), 0) >> _log2(t_new)
    col_h = lax.broadcasted_iota(jnp.int32, (rows, GROUP_W), 1) >> _log2(HEAD_DIM)
    qbd = jnp.where(row_h == col_h, qt, 0.0).astype(BF16)
    k_t = cin[0:GROUP_W, :].astype(BF16)
    v_t = cin[GROUP_W:2 * GROUP_W, :].astype(BF16)
    s_c = jnp.dot(qbd, k_t, preferred_element_type=F32)
    s_n = jnp.dot(qbd, new_t[0:GROUP_W, :].astype(BF16), preferred_element_type=F32)
    _log2(dilation)
    slope = sl_ref[:, 0:1]
    t_c = lax.broadcasted_iota(jnp.int32, s_c.shape, 0) & (t_new - 1)
    dist_c = window + t_c - lax.broadcasted_iota(jnp.int32, s_c.shape, 1)
    s_c = jnp.where((dist_c <= window) & ((dist_c & (dilation - 1)) == 0), s_c - slope * dist_c.astype(F32), NEG_INF)
    t_n = lax.broadcasted_iota(jnp.int32, s_n.shape, 0) & (t_new - 1)
    dist_n = t_n - lax.broadcasted_iota(jnp.int32, s_n.shape, 1)
    s_n = jnp.where((dist_n >= 0) & ((dist_n & (dilation - 1)) == 0), s_n - slope * dist_n.astype(F32), NEG_INF)
    m = jnp.maximum(jnp.max(s_c, axis=-1, keepdims=True), jnp.max(s_n, axis=-1, keepdims=True))
    e_c = jnp.exp(s_c - m)
    e_n = jnp.exp(s_n - m)
    den = jnp.sum(e_c, axis=-1, keepdims=True) + jnp.sum(e_n, axis=-1, keepdims=True)
    o_full = lax.dot_general(e_c.astype(BF16), v_t, (((1,), (1,)), ((), ())), preferred_element_type=F32)
    o_full = (o_full + jnp.dot(e_n.astype(BF16), vn.astype(BF16), preferred_element_type=F32)) / den
    lse = m + jnp.log(den)
    out_h = lax.broadcasted_iota(jnp.int32, (t_new, GROUP_W), 1) >> _log2(HEAD_DIM)
    o = jnp.zeros((t_new, GROUP_W), F32)
    l = jnp.zeros((t_new, GROUP_W), F32)
    for h in range(HEADS_PER_GROUP):
        rs = slice(h * t_new, (h + 1) * t_new)
        o = jnp.where(out_h == h, o_full[rs, :], o)
        l = jnp.where(out_h == h, lse[rs, :], l)
    for c in range(LANE_GROUPS):
        o_ref[c] = o[:, c * LANES:(c + 1) * LANES]
        l_ref[c] = l[:, c * LANES:(c + 1) * LANES]


def _attn_sample(z, cache, g):
    batch, window = cache.shape[0], cache.shape[1]
    n_tok = z.shape[0]
    t_new = n_tok // batch
    d = ATT_DILATIONS[g]
    c_t = jnp.transpose(cache, (0, 2, 3, 4, 1)).reshape(batch, 2 * GROUP_W, window)
    slopes = np.repeat(np.asarray(_SLOPES[g * HEADS_PER_GROUP:(g + 1) * HEADS_PER_GROUP], np.float32), t_new)
    slopes = jnp.asarray(np.broadcast_to(slopes[:, None], (HEADS_PER_GROUP * t_new, LANES)))
    col0 = COL_QKV + 3 * g

    def col(c):
        return pl.BlockSpec((t_new, GROUP_W), lambda b: (b, c))

    tok_spec = pl.BlockSpec((LANE_GROUPS, t_new, LANES), lambda b: (0, b, 0))
    tok_sds = jax.ShapeDtypeStruct((LANE_GROUPS, n_tok, LANES), F32)
    buf_spec = pl.BlockSpec((None, 2 * GROUP_W, window), lambda b: (b, 0, 0))
    o, l, cout = pl.pallas_call(
        functools.partial(_attn_sample_kernel, window=window, dilation=d),
        grid=(batch,),
        in_specs=[col(col0), col(col0 + 1), col(col0 + 2), buf_spec, pl.BlockSpec(slopes.shape, lambda b: (0, 0))],
        out_specs=[tok_spec, tok_spec, buf_spec],
        out_shape=[tok_sds, tok_sds, jax.ShapeDtypeStruct(c_t.shape, F32)],
        compiler_params=_params("parallel"),
        name=f"attn_sample_g{g}",
    )(z, z, z, c_t, slopes)
    cout = cout.reshape(batch, 2, HEADS_PER_GROUP, HEAD_DIM, window)
    return o, l, jnp.transpose(cout, (0, 4, 1, 2, 3))


def _gelu_tanh(x):
    return x * (0.5 * (1.0 + jnp.tanh(np.sqrt(2.0 / np.pi).astype(np.float32) * (x + 0.044715 * (x * x * x)))))


def _rglru_kernel(xb_ref, yg_ref, c0_ref, h0_ref, cw_ref, cb_ref, wa_ref, wi_ref, ba_ref, bi_ref, lam_ref,
                  yb_ref, cout_ref, hout_ref, xpad, a_s, u_s, h_s, *, tc, seg, reset_first):
    nb = xb_ref.shape[0]
    ngl = a_s.shape[0]
    j = pl.program_id(1)

    @pl.when(j == 0)
    def _():
        xpad[:, 0:SUBLANES, :] = c0_ref[...]
        h_s[...] = h0_ref[...]

    @pl.when(j > 0)
    def _():
        xpad[:, 0:SUBLANES, :] = xpad[:, tc:tc + SUBLANES, :]

    xpad[:, SUBLANES:SUBLANES + tc, :] = xb_ref[...].astype(F32)
    cout_ref[...] = xpad[:, tc:tc + SUBLANES, :]

    lam = lam_ref[...]
    softplus_neg = jnp.maximum(-lam, 0.0) + jnp.log1p(jnp.exp(-jnp.abs(lam)))
    first = (lax.broadcasted_iota(jnp.int32, (tc, 1), 0) == 0) & (j == 0)
    gw = wa_ref.shape[1]
    for b in range(nb):
        xc = cb_ref[...]
        for tap in range(CONV_WIDTH):
            off = SUBLANES - (CONV_WIDTH - 1) + tap
            xc = xc + xpad[b, off:off + tc, :] * cw_ref[tap:tap + 1, :]
        xcb = xc.astype(BF16)
        ra, ri = [], []
        for blk in range(wa_ref.shape[0]):
            xs = xcb[:, blk * gw:(blk + 1) * gw]
            ra.append(jnp.dot(xs, wa_ref[blk], preferred_element_type=F32))
            ri.append(jnp.dot(xs, wi_ref[blk], preferred_element_type=F32))
        r = _sigmoid(jnp.concatenate(ra, axis=-1) + ba_ref[...])
        i = _sigmoid(jnp.concatenate(ri, axis=-1) + bi_ref[...])
        log_a = -LRU_C * r * softplus_neg
        a = jnp.exp(log_a)
        mult = jnp.sqrt(1.0 - a * a)
        if reset_first:
            mult = jnp.where(first, 1.0, mult)
        u = mult * (i * xc)
        for c in range(ngl):
            a_s[c, b * seg:b * seg + tc, :] = a[:, c * LANES:(c + 1) * LANES]
            u_s[c, b * seg:b * seg + tc, :] = u[:, c * LANES:(c + 1) * LANES]

    def step(t, hs):
        idx = pl.ds(t, nb, stride=seg)
        out = []
        for c in range(ngl):
            h = a_s[c, idx, :] * hs[c] + u_s[c, idx, :]
            u_s[c, idx, :] = h
            out.append(h)
        return tuple(out)

    hs = lax.fori_loop(0, tc, step, tuple(h_s[:, c * LANES:(c + 1) * LANES] for c in range(ngl)))
    h = jnp.concatenate(hs, axis=-1)
    h_s[...] = h
    hout_ref[...] = h
    for b in range(nb):
        hb = jnp.concatenate([u_s[c, b * seg:b * seg + tc, :] for c in range(ngl)], axis=-1)
        yb_ref[b] = (_gelu_tanh(yg_ref[b].astype(F32)) * hb).astype(yb_ref.dtype)


def _rglru(z3, conv0, h0, p, reset_first):
    batch, t, _ = z3.shape
    width = h0.shape[-1]
    nb = SUBLANES
    tc = min(128, t)
    seg = tc + SUBLANES
    c0 = jnp.pad(conv0, ((0, 0), (SUBLANES - (CONV_WIDTH - 1), 0), (0, 0)))
    cw = jnp.pad(p["conv_w"], ((0, SUBLANES - CONV_WIDTH), (0, 0)))
    ngrp = p["wa"].shape[0]
    gw = p["wa"].shape[1]

    def full(shape):
        return pl.BlockSpec(shape, lambda i, j: (0,) * len(shape))

    yb, cout, hout = pl.pallas_call(
        functools.partial(_rglru_kernel, tc=tc, seg=seg, reset_first=reset_first),
        grid=(batch // nb, t // tc),
        in_specs=[
            pl.BlockSpec((nb, tc, width), lambda i, j: (i, j, 0)),
            pl.BlockSpec((nb, tc, width), lambda i, j: (i, j, 1)),
            pl.BlockSpec((nb, SUBLANES, width), lambda i, j: (i, 0, 0)),
            pl.BlockSpec((nb, width), lambda i, j: (i, 0)),
            full((SUBLANES, width)), full((1, width)),
            full((ngrp, gw, gw)), full((ngrp, gw, gw)),
            full((1, width)), full((1, width)), full((1, width)),
        ],
        out_specs=[
            pl.BlockSpec((nb, tc, width), lambda i, j: (i, j, 0)),
            pl.BlockSpec((nb, SUBLANES, width), lambda i, j: (i, 0, 0)),
            pl.BlockSpec((nb, width), lambda i, j: (i, 0)),
        ],
        out_shape=[
            jax.ShapeDtypeStruct((batch, t, width), BF16),
            jax.ShapeDtypeStruct((batch, SUBLANES, width), F32),
            jax.ShapeDtypeStruct((batch, width), F32),
        ],
        scratch_shapes=[
            pltpu.VMEM((nb, tc + SUBLANES, width), F32),
            pltpu.VMEM((width // LANES, nb * seg, LANES), F32),
            pltpu.VMEM((width // LANES, nb * seg, LANES), F32),
            pltpu.VMEM((nb, width), F32),
        ],
        compiler_params=_params("parallel", "arbitrary"),
        name="rglru",
    )(z3, z3, c0, h0, cw, p["conv_b"], p["wa"], p["wi"], p["ba"], p["bi"], p["lam"])
    return yb, cout[:, SUBLANES - (CONV_WIDTH - 1):, :], hout


def _mix_kernel(o1, o2, o3, l1, l2, l3, yb_ref, ga_ref, gb_ref, x_ref, wpa_ref, wpb_ref, wo_ref, nf_ref,
                wr_ref, br_ref, *rest):
    x1_ref, xn_ref, idx_ref, gate_ref = rest[-4:]
    parts = []
    for c in range(LANE_GROUPS):
        la, lb, lc = l1[c], l2[c], l3[c]
        m = jnp.maximum(jnp.maximum(la, lb), lc)
        ea, eb, ec = jnp.exp(la - m), jnp.exp(lb - m), jnp.exp(lc - m)
        inv = 1.0 / (ea + eb + ec)
        parts.append((ea * inv) * o1[c] + (eb * inv) * o2[c] + (ec * inv) * o3[c])
    att = jnp.concatenate(parts, axis=-1)
    y_a = jnp.dot(att.astype(BF16), wpa_ref[...], preferred_element_type=F32)
    y_b = jnp.dot(yb_ref[...], wpb_ref[...], preferred_element_type=F32)
    merged = _sigmoid(ga_ref[...].astype(F32)) * y_a + _sigmoid(gb_ref[...].astype(F32)) * y_b
    x1 = x_ref[...] + jnp.dot(merged.astype(BF16), wo_ref[...], preferred_element_type=F32)
    x1_ref[...] = x1
    ms = jnp.mean(x1 * x1, axis=-1, keepdims=True)
    xn = (x1 * lax.rsqrt(ms + RMS_EPS) * nf_ref[...]).astype(BF16)
    xn_ref[...] = xn
    logits = jnp.dot(xn, wr_ref[...], preferred_element_type=F32) + br_ref[...]
    lane = lax.broadcasted_iota(jnp.int32, logits.shape, 1)
    lane_f = lane.astype(F32)
    logits = jnp.where(lane < N_EXPERTS, logits, NEG_INF)
    vals, idxs = [], []
    for _ in range(TOP_K):
        mk = jnp.max(logits, axis=-1, keepdims=True)
        ik = jnp.min(jnp.where(logits == mk, lane_f, float(LANES)), axis=-1, keepdims=True)
        logits = jnp.where(lane_f == ik, NEG_INF, logits)
        vals.append(mk)
        idxs.append(ik)
    es = [jnp.exp(vk - vals[0]) for vk in vals]
    tot = es[0] + es[1] + es[2] + es[3]
    idx_out = jnp.zeros(logits.shape, F32)
    gate_out = jnp.zeros(logits.shape, F32)
    for kk in range(TOP_K):
        idx_out = jnp.where(lane == kk, idxs[kk], idx_out)
        gate_out = jnp.where(lane == kk, es[kk] / tot, gate_out)
    idx_ref[...] = jnp.transpose(idx_out)[0:SUBLANES, :].astype(jnp.int32)
    gate_ref[...] = gate_out


def _mix(os_, ls_, yb, z, x2d, p, n_all, row0, xn_buf=None):
    n, d = x2d.shape
    tm = min(256, n)
    assert row0 % tm == 0

    def rows(w, c=0):
        return pl.BlockSpec((tm, w), lambda i: (i, c))

    def full(a):
        return pl.BlockSpec(a.shape, lambda i: (0,) * a.ndim)

    grp = pl.BlockSpec((LANE_GROUPS, tm, LANES), lambda i: (0, i, 0))
    in_specs = [grp] * 6 + [rows(d), rows(d, 2), rows(d, 3), rows(d),
                            full(p["w_pa"]), full(p["w_pb"]), full(p["w_o"]), full(p["norm_ffn"]),
                            full(p["w_router"]), full(p["b_router"])]
    args = [*os_, *ls_, yb, z, z, x2d, p["w_pa"], p["w_pb"], p["w_o"], p["norm_ffn"], p["w_router"], p["b_router"]]
    aliases = {}
    if xn_buf is not None:
        aliases = {len(args): 1}
        in_specs.append(pl.BlockSpec(memory_space=pl.ANY))
        args.append(xn_buf)
    x1, xn, idx, gate = pl.pallas_call(
        _mix_kernel,
        grid=(n // tm,),
        in_specs=in_specs,
        out_specs=[rows(d), pl.BlockSpec((tm, d), lambda i: (i + row0 // tm, 0)),
                   pl.BlockSpec((SUBLANES, tm), lambda i: (0, i)), rows(LANES)],
        out_shape=[jax.ShapeDtypeStruct((n, d), F32), jax.ShapeDtypeStruct((n_all, d), BF16),
                   jax.ShapeDtypeStruct((SUBLANES, n), jnp.int32), jax.ShapeDtypeStruct((n, LANES), F32)],
        input_output_aliases=aliases,
        compiler_params=_params("parallel"),
        name="mix",
    )(*args)
    return x1, xn, idx, gate


def _expert_kernel(be_ref, nused_ref, x_ref, w1_ref, b1_ref, w2_ref, b2_ref, perm_ref, y_ref, w1_s, w2_s):
    i = pl.program_id(0)
    dff = w2_ref.shape[0]
    n_cb = w1_ref.shape[1] // MXU_DIM
    half = MXU_DIM // 2

    @pl.when((i == 0) | (be_ref[i] != be_ref[jnp.maximum(i - 1, 0)]))
    def _():
        for cb in range(n_cb):
            cs = slice(cb * MXU_DIM, (cb + 1) * MXU_DIM)
            w1_s[:, cs] = jnp.dot(w1_ref[:, cs].astype(BF16), perm_ref[...], preferred_element_type=F32).astype(BF16)
        w2_s[...] = w2_ref[...].astype(BF16)

    @pl.when(i < nused_ref[0])
    def _():
        h = jnp.dot(x_ref[...], w1_s[...], preferred_element_type=F32) + b1_ref[...]
        acts = []
        for cb in range(n_cb):
            x_glu = jnp.minimum(h[:, cb * MXU_DIM:cb * MXU_DIM + half], SWIGLU_LIMIT)
            x_lin = jnp.clip(h[:, cb * MXU_DIM + half:(cb + 1) * MXU_DIM], -SWIGLU_LIMIT, SWIGLU_LIMIT)
            acts.append(x_glu * _sigmoid(SWIGLU_ALPHA * x_glu) * (x_lin + 1.0))
        act = jnp.concatenate(acts, axis=-1)
        assert act.shape[-1] == dff
        y = jnp.dot(act.astype(BF16), w2_s[...], preferred_element_type=F32) + b2_ref[...]
        y_ref[...] = y.astype(y_ref.dtype)

    @pl.when(i >= nused_ref[0])
    def _():
        y_ref[...] = jnp.zeros_like(y_ref)


def _experts(x_rows, block_expert, n_used, p):
    n_rows, d = x_rows.shape
    n_blocks = n_rows // MOE_BLOCK
    dff2 = p["w1"].shape[-1]
    half = MXU_DIM // 2
    o = np.arange(MXU_DIM)
    perm = np.zeros((MXU_DIM, MXU_DIM), np.float32)
    perm[np.where(o < half, 2 * o, 2 * (o - half) + 1), o] = 1.0
    grid_spec = pltpu.PrefetchScalarGridSpec(
        num_scalar_prefetch=2,
        grid=(n_blocks,),
        in_specs=[
            pl.BlockSpec((MOE_BLOCK, d), lambda i, be, nu: (i, 0)),
            pl.BlockSpec((None, d, dff2), lambda i, be, nu: (be[i], 0, 0)),
            pl.BlockSpec((None, 1, dff2), lambda i, be, nu: (be[i], 0, 0)),
            pl.BlockSpec((None, dff2 // 2, d), lambda i, be, nu: (be[i], 0, 0)),
            pl.BlockSpec((None, 1, d), lambda i, be, nu: (be[i], 0, 0)),
            pl.BlockSpec((MXU_DIM, MXU_DIM), lambda i, be, nu: (0, 0)),
        ],
        out_specs=pl.BlockSpec((MOE_BLOCK, d), lambda i, be, nu: (i, 0)),
        scratch_shapes=[pltpu.VMEM((d, dff2), BF16), pltpu.VMEM((dff2 // 2, d), BF16)],
    )
    return pl.pallas_call(
        _expert_kernel,
        grid_spec=grid_spec,
        out_shape=jax.ShapeDtypeStruct((n_rows, d), BF16),
        compiler_params=_params("arbitrary"),
        name="experts",
    )(block_expert, n_used, x_rows, p["w1"], p["b1"], p["w2"], p["b2"], jnp.asarray(perm, BF16))


def _combine_kernel(y0, y1, y2, y3, gate_ref, x1_ref, nf_ref, o_ref):
    g = gate_ref[...]
    y = jnp.zeros(x1_ref.shape, F32)
    for kk, y_ref in enumerate((y0, y1, y2, y3)):
        y = y + y_ref[...].astype(F32) * g[:, kk:kk + 1]
    x = x1_ref[...] + y
    ms = jnp.mean(x * x, axis=-1, keepdims=True)
    o_ref[...] = x * lax.rsqrt(ms + RMS_EPS) * nf_ref[...]


def _combine(ys, gate, x1, norm_final):
    n, d = x1.shape
    tm = min(256, n)
    row = pl.BlockSpec((tm, d), lambda i: (i, 0))
    return pl.pallas_call(
        _combine_kernel,
        grid=(n // tm,),
        in_specs=[row] * TOP_K + [pl.BlockSpec((tm, LANES), lambda i: (i, 0)), row,
                                  pl.BlockSpec((1, d), lambda i: (0, 0))],
        out_specs=row,
        out_shape=jax.ShapeDtypeStruct((n, d), F32),
        compiler_params=_params("parallel"),
        name="combine",
    )(*ys, gate, x1, norm_final)


def _rank_kernel(idx_ref, tri_ref, rank_ref, cnt_ref, carry):
    @pl.when(pl.program_id(0) == 0)
    def _():
        carry[...] = jnp.zeros_like(carry)

    idx = idx_ref[...]
    tm = idx.shape[1]
    expert = lax.broadcasted_iota(jnp.int32, (N_EXPERTS, tm), 0)
    hits = [expert == idx[kk:kk + 1, :] for kk in range(TOP_K)]
    onehot = jnp.zeros((N_EXPERTS, tm), F32)
    for hit in hits:
        onehot = onehot + jnp.where(hit, 1.0, 0.0)
    before = jnp.dot(onehot.astype(BF16), tri_ref[...], preferred_element_type=F32) + carry[:, 0:1]
    row = lax.broadcasted_iota(jnp.int32, idx.shape, 0)
    out = jnp.zeros(idx.shape, F32)
    for kk, hit in enumerate(hits):
        out = jnp.where(row == kk, jnp.sum(jnp.where(hit, before, 0.0), axis=0, keepdims=True), out)
    rank_ref[...] = out.astype(jnp.int32)
    carry[...] = carry[...] + jnp.sum(onehot, axis=1, keepdims=True)
    cnt_ref[...] = carry[...].astype(jnp.int32)


def _rank(idx_t):
    n = idx_t.shape[1]
    tm = 512
    while n % tm:
        tm //= 2
    tri = jnp.asarray(np.triu(np.ones((tm, tm), np.float32), 1), BF16)
    return pl.pallas_call(
        _rank_kernel,
        grid=(n // tm,),
        in_specs=[pl.BlockSpec((SUBLANES, tm), lambda i: (0, i)), pl.BlockSpec((tm, tm), lambda i: (0, 0))],
        out_specs=[pl.BlockSpec((SUBLANES, tm), lambda i: (0, i)), pl.BlockSpec((N_EXPERTS, LANES), lambda i: (0, 0))],
        out_shape=[jax.ShapeDtypeStruct((SUBLANES, n), jnp.int32), jax.ShapeDtypeStruct((N_EXPERTS, LANES), jnp.int32)],
        scratch_shapes=[pltpu.VMEM((N_EXPERTS, LANES), F32)],
        compiler_params=_params("arbitrary"),
        name="rank",
    )(idx_t, tri)


def _route(idx_t):
    n_tok = idx_t.shape[1]
    n_assign = n_tok * TOP_K
    tok_bits = max(n_tok - 1, 1).bit_length()
    assert N_EXPERTS << tok_bits < 2 ** 31
    rank_t, cnt = _rank(idx_t)
    counts = cnt[:, 0]
    e_t = idx_t[:TOP_K]
    keys = jnp.sort(((e_t << tok_bits) + jnp.arange(n_tok, dtype=jnp.int32)[None, :]).reshape(-1))
    padded = (counts + MOE_BLOCK - 1) // MOE_BLOCK * MOE_BLOCK
    starts = jnp.cumsum(counts) - counts
    pends = jnp.cumsum(padded)
    pstarts = pends - padded
    experts = jnp.arange(N_EXPERTS, dtype=jnp.int32)[:, None, None]
    dest_t = jnp.sum(jnp.where(e_t[None] == experts, pstarts[:, None, None], 0), axis=0) + rank_t[:TOP_K]
    n_blocks = -(-(n_assign + N_EXPERTS * (MOE_BLOCK - 1)) // MOE_BLOCK)
    n_rows = n_blocks * MOE_BLOCK
    block_start = jnp.arange(n_blocks, dtype=jnp.int32) * MOE_BLOCK
    block_expert = jnp.minimum(jnp.sum((pends[None, :] <= block_start[:, None]).astype(jnp.int32), axis=1),
                               N_EXPERTS - 1)
    local = block_start - pstarts[block_expert]
    first = jnp.repeat(starts[block_expert] + local, MOE_BLOCK)
    live = jnp.repeat(counts[block_expert] - local, MOE_BLOCK)
    within = jnp.tile(jnp.arange(MOE_BLOCK, dtype=jnp.int32), n_blocks)
    src = keys[jnp.clip(first + within, 0, n_assign - 1)] & ((1 << tok_bits) - 1)
    row = jnp.arange(n_rows, dtype=jnp.int32)
    row_tok = jnp.where(within < live, src, row % n_tok)
    n_used = (pends[-1] // MOE_BLOCK).astype(jnp.int32).reshape(1)
    return row_tok, dest_t, block_expert, n_used


def _mixer(x, p, caches, conv0, h0, n_all, row0, xn_buf):
    batch, t, d = x.shape
    prompt = caches is None
    x2d = x.reshape(batch * t, d)
    os_, ls_, new_kv = [], [], []
    if prompt:
        nat = N_NATURAL_BLOCKS * GROUP_W
        z, zd = _in_proj_prompt(x2d, p["norm_mix"], p["w_in"], t, nat, ATT_DILATIONS[1:])
        zq = [z.reshape(batch, 1, t, nat)] + list(zd)
        for g in range(N_GROUPS):
            dil = ATT_DILATIONS[g]
            o, l = _attn_prompt(zq[g], g, COL_QKV if g == 0 else 0)
            keep = min(ATT_WINDOWS[g], t)
            if g == 0:
                kcol = (COL_QKV + 1) * GROUP_W
                last = z.reshape(batch, t, nat)[:, t - keep:, kcol:kcol + 2 * GROUP_W]
            else:
                last = zq[g][:, :, (t - keep) // dil:, GROUP_W:3 * GROUP_W]
                last = jnp.swapaxes(last, 1, 2).reshape(batch, keep, 2 * GROUP_W)
            kv = last.astype(F32).reshape(batch, keep, 2, HEADS_PER_GROUP, HEAD_DIM)
            os_.append(o)
            ls_.append(l)
            new_kv.append(kv[None])
        conv0 = jnp.zeros((batch, CONV_WIDTH - 1, d), F32)
        h0 = jnp.zeros((batch, d), F32)
    else:
        z = _in_proj(x2d, p["norm_mix"], p["w_in"], F32, 4)
        for g in range(N_GROUPS):
            o, l, kv = _attn_sample(z, caches[g], g)
            os_.append(o)
            ls_.append(l)
            new_kv.append(kv[None])
    yb, new_conv, h_last = _rglru(z.reshape(batch, t, -1), conv0, h0, p, prompt)
    x1, xn, idx, gate = _mix(os_, ls_, yb.reshape(batch * t, d), z, x2d, p, n_all, row0, xn_buf)
    return x1, xn, idx, gate, new_kv, new_conv[None], h_last[None]


def kernel(x_prompt, x_sample, cache_kv_g1, cache_kv_g2, cache_kv_g3, state_conv, state_h, norm_mix, w_in, w_pa,
           w_pb, w_o, conv_w, conv_b, lru_wa, lru_ba, lru_wi, lru_bi, lru_lambda, norm_ffn, w_router, b_router,
           w1, b1, w2, b2, norm_final):
    d = x_prompt.shape[-1]
    gsz = N_GROUPS * GROUP_W

    def block_diag(w):
        per = MXU_DIM // LRU_BLOCK_W
        w = w.reshape(-1, per, LRU_BLOCK_W, LRU_BLOCK_W)
        eye = jnp.eye(per, dtype=w.dtype)
        return jnp.einsum("gacd,ab->gacbd", w, eye).reshape(-1, MXU_DIM, MXU_DIM).astype(BF16)

    w_in0 = w_in[0]
    qkv = [w_in0[:, which * gsz + g * GROUP_W: which * gsz + (g + 1) * GROUP_W]
           for g in range(N_GROUPS) for which in range(3)]
    n_e, dff2 = b1.shape[1], b1.shape[2]
    half = MXU_DIM // 2
    p = dict(
        norm_mix=norm_mix[0][None], norm_ffn=norm_ffn[0][None],
        w_in=jnp.concatenate([w_in0[:, 3 * gsz:]] + qkv, axis=1).astype(BF16),
        w_pa=w_pa[0].astype(BF16), w_pb=w_pb[0].astype(BF16), w_o=w_o[0].astype(BF16),
        conv_w=conv_w[0], conv_b=conv_b[0][None],
        wa=block_diag(lru_wa[0]), wi=block_diag(lru_wi[0]),
        ba=lru_ba[0].reshape(1, d), bi=lru_bi[0].reshape(1, d), lam=lru_lambda[0][None],
        w_router=jnp.pad(w_router[0], ((0, 0), (0, LANES - N_EXPERTS))).astype(BF16),
        b_router=jnp.pad(b_router[0], (0, LANES - N_EXPERTS))[None],
        w1=w1[0], w2=w2[0],
        b1=b1[0].reshape(n_e, dff2 // MXU_DIM, half, 2).transpose(0, 1, 3, 2).reshape(n_e, 1, dff2),
        b2=b2[0][:, None, :],
    )

    n_p = x_prompt.shape[0] * x_prompt.shape[1]
    n_all = n_p + x_sample.shape[0] * x_sample.shape[1]
    x1_p, xn_all, idx_p, gate_p, kv_p, conv_p, h_p = _mixer(x_prompt, p, None, None, None, n_all, 0, None)
    x1_s, xn_all, idx_s, gate_s, kv_s, conv_s, h_s = _mixer(
        x_sample, p, [cache_kv_g1[0], cache_kv_g2[0], cache_kv_g3[0]], state_conv[0], state_h[0], n_all, n_p, xn_all)

    row_tok, dest, block_expert, n_used = _route(jnp.concatenate([idx_p, idx_s], axis=1))
    y_rows = _experts(xn_all[row_tok], block_expert, n_used, p)
    nf = norm_final[None]
    y_p = _combine([y_rows[dest[kk, :n_p]] for kk in range(TOP_K)], gate_p, x1_p, nf).reshape(x_prompt.shape)
    y_s = _combine([y_rows[dest[kk, n_p:]] for kk in range(TOP_K)], gate_s, x1_s, nf).reshape(x_sample.shape)
    return (y_p, y_s, kv_p[0], kv_p[1], kv_p[2], conv_p, h_p, kv_s[0], kv_s[1], kv_s[2], conv_s, h_s)
```

```python
import functools

import numpy as np
import jax
import jax.numpy as jnp
from jax import lax
from jax.experimental import pallas as pl
from jax.experimental.pallas import tpu as pltpu

F32 = jnp.float32
BF16 = jnp.bfloat16

N_GROUPS = 3
HEADS_PER_GROUP = 8
HEAD_DIM = 64
GROUP_W = HEADS_PER_GROUP * HEAD_DIM
ATT_WINDOWS = (128, 512, 2048)
ATT_DILATIONS = (1, 4, 16)
ATT_BLOCK = 128
N_ATT_HEADS = N_GROUPS * HEADS_PER_GROUP
CONV_WIDTH = 4
LRU_C = 8.0
LRU_BLOCK_W = 64
N_EXPERTS = 32
TOP_K = 4
SWIGLU_ALPHA = 1.702
SWIGLU_LIMIT = 7.0
RMS_EPS = 1e-6
NEG_INF = float("-inf")

LANES = 128
SUBLANES = 8
MXU_DIM = 256
VMEM_LIMIT_BYTES = 56 * 1024 * 1024

COL_QKV = 8
N_NATURAL_BLOCKS = COL_QKV + 3
LANE_GROUPS = GROUP_W // LANES
MOE_BLOCK = 512
PROJ_TILE = 512
ATT_UNITS = 2

_SLOPES = [float(np.float32(2.0 ** (-8.0 * (i + 1) / N_ATT_HEADS))) for i in range(N_ATT_HEADS)]


def _params(*sem):
    return pltpu.CompilerParams(dimension_semantics=sem, vmem_limit_bytes=VMEM_LIMIT_BYTES)


def _sigmoid(x):
    return 0.5 * jnp.tanh(0.5 * x) + 0.5


def _log2(n):
    assert n > 0 and n & (n - 1) == 0, n
    return n.bit_length() - 1


def _rmsnorm_bf16(x, gain):
    ms = jnp.mean(x * x, axis=-1, keepdims=True)
    return (x * lax.rsqrt(ms + RMS_EPS) * gain).astype(BF16)


def _in_proj_kernel(x_ref, g_ref, w_ref, o_ref, xn_ref):
    @pl.when(pl.program_id(1) == 0)
    def _():
        xn_ref[...] = _rmsnorm_bf16(x_ref[...], g_ref[...])

    o_ref[...] = jnp.dot(xn_ref[...], w_ref[...], preferred_element_type=F32).astype(o_ref.dtype)


def _in_proj(x2d, gain, w, out_dtype, n_col_tiles):
    n, d = x2d.shape
    cols = w.shape[1]
    tm = min(PROJ_TILE, n)
    tn = cols // n_col_tiles
    return pl.pallas_call(
        _in_proj_kernel,
        grid=(n // tm, n_col_tiles),
        in_specs=[
            pl.BlockSpec((tm, d), lambda i, j: (i, 0)),
            pl.BlockSpec((1, d), lambda i, j: (0, 0)),
            pl.BlockSpec((d, tn), lambda i, j: (0, j)),
        ],
        out_specs=pl.BlockSpec((tm, tn), lambda i, j: (i, j)),
        out_shape=jax.ShapeDtypeStruct((n, cols), out_dtype),
        scratch_shapes=[pltpu.VMEM((tm, d), BF16)],
        compiler_params=_params("parallel", "arbitrary"),
        name="in_proj",
    )(x2d, gain, w)


def _in_proj_prompt_kernel(x_ref, g_ref, w_ref, *refs, dilations, n_chunks):
    n_dil = len(dilations)
    perm_refs, zn_ref, zd_refs = refs[:n_dil], refs[n_dil], refs[n_dil + 1:]
    xn = _rmsnorm_bf16(x_ref[...], g_ref[...])
    nat = zn_ref.shape[1]
    cw = nat // n_chunks
    for c in range(n_chunks):
        cs = slice(c * cw, (c + 1) * cw)
        zn_ref[:, cs] = jnp.dot(xn, w_ref[:, cs], preferred_element_type=F32).astype(zn_ref.dtype)
    col = nat
    for perm_ref, zd_ref, dil in zip(perm_refs, zd_refs, dilations):
        width = zd_ref.shape[-1]
        xp = jnp.dot(perm_ref[...], xn, preferred_element_type=F32).astype(BF16)
        res = jnp.dot(xp, w_ref[:, col:col + width], preferred_element_type=F32).astype(zd_ref.dtype)
        per = res.shape[0] // dil
        for r in range(dil):
            zd_ref[r] = res[r * per:(r + 1) * per, :]
        col += width


def _in_proj_prompt(x2d, gain, w, seq, nat, dilations):
    n, d = x2d.shape
    cols = w.shape[1]
    tm = PROJ_TILE // 2
    width = 3 * GROUP_W
    assert nat + width * len(dilations) == cols and seq % tm == 0
    tiles_per_seq = seq // tm
    perms = []
    for dil in dilations:
        per = tm // dil
        o = np.arange(tm)
        perm = np.zeros((tm, tm), np.float32)
        perm[o, (o % per) * dil + o // per] = 1.0
        perms.append(jnp.asarray(perm, BF16))
    const = lambda shape: pl.BlockSpec(shape, lambda i: (0,) * len(shape))
    outs = pl.pallas_call(
        functools.partial(_in_proj_prompt_kernel, dilations=tuple(dilations), n_chunks=2),
        grid=(n // tm,),
        in_specs=[pl.BlockSpec((tm, d), lambda i: (i, 0)), const((1, d)),
                  pl.BlockSpec((d, cols), lambda i: (0, 0), pipeline_mode=pl.Buffered(1))]
        + [const((tm, tm))] * len(dilations),
        out_specs=[pl.BlockSpec((tm, nat), lambda i: (i, 0))]
        + [pl.BlockSpec((None, dil, tm // dil, width), lambda i: (i // tiles_per_seq, 0, i % tiles_per_seq, 0))
           for dil in dilations],
        out_shape=[jax.ShapeDtypeStruct((n, nat), BF16)]
        + [jax.ShapeDtypeStruct((n // seq, dil, seq // dil, width), BF16) for dil in dilations],
        compiler_params=_params("parallel"),
        name="in_proj_prompt",
    )(x2d, gain, w, *perms)
    return outs[0], outs[1:]


def _attn_prompt_kernel(q_ref, kp_ref, kc_ref, vp_ref, vc_ref, b_ref, o_ref, l_ref, *, dilation, n_sub, units):
    blk = ATT_BLOCK
    j = pl.program_id(1)
    low = lax.broadcasted_iota(jnp.int32, (blk, LANES), 1) < HEAD_DIM

    def unit(r, u):
        own = slice(u * blk, (u + 1) * blk)
        q = q_ref[r, own, :] * (HEAD_DIM ** -0.5)
        if u == 0:
            k = jnp.concatenate([kp_ref[r], kc_ref[r, own, :]], axis=0)
            v = jnp.concatenate([vp_ref[r], vc_ref[r, own, :]], axis=0)
            bias = b_ref.at[jnp.minimum(j, 1)]
        else:
            both = slice((u - 1) * blk, (u + 1) * blk)
            k, v = kc_ref[r, both, :], vc_ref[r, both, :]
            bias = b_ref.at[1]
        rows = pl.ds(u * blk * dilation + r, blk, stride=dilation) if dilation > 1 else own
        for p in range(LANE_GROUPS):
            cs = slice(p * LANES, (p + 1) * LANES)
            qg, kg, vg = q[:, cs], k[:, cs], v[:, cs]
            outs, lses = [], []
            for hh in range(2):
                sel = low if hh == 0 else jnp.logical_not(low)
                qm = jnp.where(sel, qg, jnp.zeros_like(qg))
                s = lax.dot_general(qm, kg, (((1,), (1,)), ((), ())), preferred_element_type=F32)
                s = s + bias[2 * p + hh]
                m = jnp.max(s, axis=-1, keepdims=True)
                e = jnp.exp(s - m)
                den = jnp.sum(e, axis=-1, keepdims=True)
                outs.append(jnp.dot(e.astype(BF16), vg, preferred_element_type=F32) / den)
                lses.append(m + jnp.log(den))
            o_ref[p, rows, :] = jnp.where(low, outs[0], outs[1])
            l_ref[p, rows, :] = jnp.where(low, lses[0], lses[1])

    per_r = min(units, n_sub)
    r_per_body = units // per_r
    assert n_sub % per_r == 0 and dilation % r_per_body == 0

    def body(i, carry):
        for rr in range(r_per_body):
            for u in range(n_sub):
                unit(i * r_per_body + rr, u)
        return carry

    if dilation == r_per_body:
        body(0, 0)
    else:
        lax.fori_loop(0, dilation // r_per_body, body, 0)


def _band_bias(g):
    blk, d = ATT_BLOCK, ATT_DILATIONS[g]
    qq = np.arange(blk)[:, None]
    kk = np.arange(2 * blk)[None, :]
    step = qq + blk - kk
    band = (step >= 0) & (step <= blk)
    slopes = np.asarray(_SLOPES[g * HEADS_PER_GROUP:(g + 1) * HEADS_PER_GROUP], np.float32)
    bias = -slopes[:, None, None] * (step * d).astype(np.float32)[None]
    out = np.empty((2, HEADS_PER_GROUP, blk, 2 * blk), np.float32)
    out[0] = np.where(band & (kk >= blk), bias, -np.inf)
    out[1] = np.where(band, bias, -np.inf)
    return jnp.asarray(out)


def _attn_prompt(zq, g, col0):
    d = ATT_DILATIONS[g]
    batch, _, n, _ = zq.shape
    n_sub = max(1, ATT_UNITS // d)
    nb = n // (ATT_BLOCK * n_sub)
    span = ATT_BLOCK * n_sub

    def cur(c):
        return pl.BlockSpec((None, d, span, GROUP_W), lambda b, j: (b, 0, j, c))

    def prev(c):
        return pl.BlockSpec((None, d, ATT_BLOCK, GROUP_W), lambda b, j: (b, 0, jnp.maximum(j * n_sub - 1, 0), c))

    out_spec = pl.BlockSpec((LANE_GROUPS, span * d, LANES), lambda b, j: (0, b * nb + j, 0))
    out_sds = jax.ShapeDtypeStruct((LANE_GROUPS, batch * n * d, LANES), F32)
    bias = _band_bias(g)
    return pl.pallas_call(
        functools.partial(_attn_prompt_kernel, dilation=d, n_sub=n_sub, units=ATT_UNITS),
        grid=(batch, nb),
        in_specs=[cur(col0), prev(col0 + 1), cur(col0 + 1), prev(col0 + 2), cur(col0 + 2),
                  pl.BlockSpec(bias.shape, lambda b, j: (0, 0, 0, 0))],
        out_specs=[out_spec, out_spec],
        out_shape=[out_sds, out_sds],
        compiler_params=_params("parallel", "arbitrary"),
        name=f"attn_prompt_g{g}",
    )(zq, zq, zq, zq, zq, bias)


def _attn_sample_kernel(q_ref, k_ref, v_ref, c_ref, sl_ref, o_ref, l_ref, cout_ref, *, window, dilation):
    t_new = q_ref.shape[0]
    rows = HEADS_PER_GROUP * t_new
    q = q_ref[...] * (HEAD_DIM ** -0.5)
    pad = jnp.zeros((LANES - t_new, GROUP_W), F32)
    kn = jnp.concatenate([k_ref[...], pad], axis=0)
    vn = jnp.concatenate([v_ref[...], pad], axis=0)
    new_t = jnp.concatenate([kn.T, vn.T], axis=0)
    cin = c_ref[...]
    cout_ref[...] = pltpu.roll(cin, window - t_new, axis=1)
    tail = cout_ref[:, window - LANES:window]
    lane = lax.broadcasted_iota(jnp.int32, tail.shape, 1)
    cout_ref[:, window - LANES:window] = jnp.where(lane >= LANES - t_new,
                                                   pltpu.roll(new_t, LANES - t_new, axis=1), tail)
    qt = jnp.concatenate([q] * HEADS_PER_GROUP, axis=0)
    row_h = lax.broadcasted_iota(jnp.int32, (rows, GROUP_W), 0) >> _log2(t_new)
    col_h = lax.broadcasted_iota(jnp.int32, (rows, GROUP_W), 1) >> _log2(HEAD_DIM)
    qbd = jnp.where(row_h == col_h, qt, 0.0).astype(BF16)
    k_t = cin[0:GROUP_W, :].astype(BF16)
    v_t = cin[GROUP_W:2 * GROUP_W, :].astype(BF16)
    s_c = jnp.dot(qbd, k_t, preferred_element_type=F32)
    s_n = jnp.dot(qbd, new_t[0:GROUP_W, :].astype(BF16), preferred_element_type=F32)
    _log2(dilation)
    slope = sl_ref[:, 0:1]
    t_c = lax.broadcasted_iota(jnp.int32, s_c.shape, 0) & (t_new - 1)
    dist_c = window + t_c - lax.broadcasted_iota(jnp.int32, s_c.shape, 1)
    s_c = jnp.where((dist_c <= window) & ((dist_c & (dilation - 1)) == 0), s_c - slope * dist_c.astype(F32), NEG_INF)
    t_n = lax.broadcasted_iota(jnp.int32, s_n.shape, 0) & (t_new - 1)
    dist_n = t_n - lax.broadcasted_iota(jnp.int32, s_n.shape, 1)
    s_n = jnp.where((dist_n >= 0) & ((dist_n & (dilation - 1)) == 0), s_n - slope * dist_n.astype(F32), NEG_INF)
    m = jnp.maximum(jnp.max(s_c, axis=-1, keepdims=True), jnp.max(s_n, axis=-1, keepdims=True))
    e_c = jnp.exp(s_c - m)
    e_n = jnp.exp(s_n - m)
    den = jnp.sum(e_c, axis=-1, keepdims=True) + jnp.sum(e_n, axis=-1, keepdims=True)
    o_full = lax.dot_general(e_c.astype(BF16), v_t, (((1,), (1,)), ((), ())), preferred_element_type=F32)
    o_full = (o_full + jnp.dot(e_n.astype(BF16), vn.astype(BF16), preferred_element_type=F32)) / den
    lse = m + jnp.log(den)
    out_h = lax.broadcasted_iota(jnp.int32, (t_new, GROUP_W), 1) >> _log2(HEAD_DIM)
    o = jnp.zeros((t_new, GROUP_W), F32)
    l = jnp.zeros((t_new, GROUP_W), F32)
    for h in range(HEADS_PER_GROUP):
        rs = slice(h * t_new, (h + 1) * t_new)
        o = jnp.where(out_h == h, o_full[rs, :], o)
        l = jnp.where(out_h == h, lse[rs, :], l)
    for c in range(LANE_GROUPS):
        o_ref[c] = o[:, c * LANES:(c + 1) * LANES]
        l_ref[c] = l[:, c * LANES:(c + 1) * LANES]


def _attn_sample(z, cache, g):
    batch, window = cache.shape[0], cache.shape[1]
    n_tok = z.shape[0]
    t_new = n_tok // batch
    d = ATT_DILATIONS[g]
    c_t = jnp.transpose(cache, (0, 2, 3, 4, 1)).reshape(batch, 2 * GROUP_W, window)
    slopes = np.repeat(np.asarray(_SLOPES[g * HEADS_PER_GROUP:(g + 1) * HEADS_PER_GROUP], np.float32), t_new)
    slopes = jnp.asarray(np.broadcast_to(slopes[:, None], (HEADS_PER_GROUP * t_new, LANES)))
    col0 = COL_QKV + 3 * g

    def col(c):
        return pl.BlockSpec((t_new, GROUP_W), lambda b: (b, c))

    tok_spec = pl.BlockSpec((LANE_GROUPS, t_new, LANES), lambda b: (0, b, 0))
    tok_sds = jax.ShapeDtypeStruct((LANE_GROUPS, n_tok, LANES), F32)
    buf_spec = pl.BlockSpec((None, 2 * GROUP_W, window), lambda b: (b, 0, 0))
    o, l, cout = pl.pallas_call(
        functools.partial(_attn_sample_kernel, window=window, dilation=d),
        grid=(batch,),
        in_specs=[col(col0), col(col0 + 1), col(col0 + 2), buf_spec, pl.BlockSpec(slopes.shape, lambda b: (0, 0))],
        out_specs=[tok_spec, tok_spec, buf_spec],
        out_shape=[tok_sds, tok_sds, jax.ShapeDtypeStruct(c_t.shape, F32)],
        compiler_params=_params("parallel"),
        name=f"attn_sample_g{g}",
    )(z, z, z, c_t, slopes)
    cout = cout.reshape(batch, 2, HEADS_PER_GROUP, HEAD_DIM, window)
    return o, l, jnp.transpose(cout, (0, 4, 1, 2, 3))


def _gelu_tanh(x):
    c1 = float(np.sqrt(2.0 / np.pi))
    return x * (0.5 * jnp.tanh(x * (c1 + (c1 * 0.044715) * (x * x))) + 0.5)


def _rglru_kernel(xb_ref, yg_ref, c0_ref, h0_ref, cw_ref, cb_ref, wa_ref, wi_ref, ba_ref, bi_ref, lam_ref,
                  yb_ref, cout_ref, hout_ref, xpad, a_s, u_s, h_s, *, tc, seg, reset_first):
    nb = xb_ref.shape[0]
    ngl = a_s.shape[0]
    j = pl.program_id(1)

    @pl.when(j == 0)
    def _():
        xpad[:, 0:SUBLANES, :] = c0_ref[...]
        h_s[...] = h0_ref[...]

    @pl.when(j > 0)
    def _():
        xpad[:, 0:SUBLANES, :] = xpad[:, tc:tc + SUBLANES, :]

    xpad[:, SUBLANES:SUBLANES + tc, :] = xb_ref[...].astype(F32)
    cout_ref[...] = xpad[:, tc:tc + SUBLANES, :]

    lam = lam_ref[...]
    softplus_neg = jnp.maximum(-lam, 0.0) + jnp.log1p(jnp.exp(-jnp.abs(lam)))
    first = (lax.broadcasted_iota(jnp.int32, (tc, 1), 0) == 0) & (j == 0)
    gw = wa_ref.shape[1]
    for b in range(nb):
        xc = cb_ref[...]
        for tap in range(CONV_WIDTH):
            off = SUBLANES - (CONV_WIDTH - 1) + tap
            xc = xc + xpad[b, off:off + tc, :] * cw_ref[tap:tap + 1, :]
        xcb = xc.astype(BF16)
        ra, ri = [], []
        for blk in range(wa_ref.shape[0]):
            xs = xcb[:, blk * gw:(blk + 1) * gw]
            ra.append(jnp.dot(xs, wa_ref[blk], preferred_element_type=F32))
            ri.append(jnp.dot(xs, wi_ref[blk], preferred_element_type=F32))
        r = _sigmoid(jnp.concatenate(ra, axis=-1) + ba_ref[...])
        i = _sigmoid(jnp.concatenate(ri, axis=-1) + bi_ref[...])
        log_a = -LRU_C * r * softplus_neg
        a = jnp.exp(log_a)
        w = 1.0 - a * a
        mult = jnp.where(w > 0.0, w * lax.rsqrt(w), 0.0)
        if reset_first:
            mult = jnp.where(first, 1.0, mult)
        u = mult * (i * xc)
        for c in range(ngl):
            a_s[c, b * seg:b * seg + tc, :] = a[:, c * LANES:(c + 1) * LANES]
            u_s[c, b * seg:b * seg + tc, :] = u[:, c * LANES:(c + 1) * LANES]

    def step(t, hs):
        idx = pl.ds(t, nb, stride=seg)
        out = []
        for c in range(ngl):
            h = a_s[c, idx, :] * hs[c] + u_s[c, idx, :]
            u_s[c, idx, :] = h
            out.append(h)
        return tuple(out)

    hs = lax.fori_loop(0, tc, step, tuple(h_s[:, c * LANES:(c + 1) * LANES] for c in range(ngl)))
    h = jnp.concatenate(hs, axis=-1)
    h_s[...] = h
    hout_ref[...] = h
    for b in range(nb):
        hb = jnp.concatenate([u_s[c, b * seg:b * seg + tc, :] for c in range(ngl)], axis=-1)
        yb_ref[b] = (_gelu_tanh(yg_ref[b].astype(F32)) * hb).astype(yb_ref.dtype)


def _rglru(z3, conv0, h0, p, reset_first):
    batch, t, _ = z3.shape
    width = h0.shape[-1]
    nb = SUBLANES
    tc = min(128, t)
    seg = tc + SUBLANES
    c0 = jnp.pad(conv0, ((0, 0), (SUBLANES - (CONV_WIDTH - 1), 0), (0, 0)))
    cw = jnp.pad(p["conv_w"], ((0, SUBLANES - CONV_WIDTH), (0, 0)))
    ngrp = p["wa"].shape[0]
    gw = p["wa"].shape[1]

    def full(shape):
        return pl.BlockSpec(shape, lambda i, j: (0,) * len(shape))

    yb, cout, hout = pl.pallas_call(
        functools.partial(_rglru_kernel, tc=tc, seg=seg, reset_first=reset_first),
        grid=(batch // nb, t // tc),
        in_specs=[
            pl.BlockSpec((nb, tc, width), lambda i, j: (i, j, 0)),
            pl.BlockSpec((nb, tc, width), lambda i, j: (i, j, 1)),
            pl.BlockSpec((nb, SUBLANES, width), lambda i, j: (i, 0, 0)),
            pl.BlockSpec((nb, width), lambda i, j: (i, 0)),
            full((SUBLANES, width)), full((1, width)),
            full((ngrp, gw, gw)), full((ngrp, gw, gw)),
            full((1, width)), full((1, width)), full((1, width)),
        ],
        out_specs=[
            pl.BlockSpec((nb, tc, width), lambda i, j: (i, j, 0)),
            pl.BlockSpec((nb, SUBLANES, width), lambda i, j: (i, 0, 0)),
            pl.BlockSpec((nb, width), lambda i, j: (i, 0)),
        ],
        out_shape=[
            jax.ShapeDtypeStruct((batch, t, width), BF16),
            jax.ShapeDtypeStruct((batch, SUBLANES, width), F32),
            jax.ShapeDtypeStruct((batch, width), F32),
        ],
        scratch_shapes=[
            pltpu.VMEM((nb, tc + SUBLANES, width), F32),
            pltpu.VMEM((width // LANES, nb * seg, LANES), F32),
            pltpu.VMEM((width // LANES, nb * seg, LANES), F32),
            pltpu.VMEM((nb, width), F32),
        ],
        compiler_params=_params("parallel", "arbitrary"),
        name="rglru",
    )(z3, z3, c0, h0, cw, p["conv_b"], p["wa"], p["wi"], p["ba"], p["bi"], p["lam"])
    return yb, cout[:, SUBLANES - (CONV_WIDTH - 1):, :], hout


def _mix_kernel(o1, o2, o3, l1, l2, l3, yb_ref, ga_ref, gb_ref, x_ref, wpa_ref, wpb_ref, wo_ref, nf_ref,
                wr_ref, br_ref, *rest):
    x1_ref, xn_ref, idx_ref, gate_ref = rest[-4:]
    parts = []
    for c in range(LANE_GROUPS):
        la, lb, lc = l1[c], l2[c], l3[c]
        m = jnp.maximum(jnp.maximum(la, lb), lc)
        ea, eb, ec = jnp.exp(la - m), jnp.exp(lb - m), jnp.exp(lc - m)
        inv = 1.0 / (ea + eb + ec)
        parts.append((ea * inv) * o1[c] + (eb * inv) * o2[c] + (ec * inv) * o3[c])
    att = jnp.concatenate(parts, axis=-1)
    y_a = jnp.dot(att.astype(BF16), wpa_ref[...], preferred_element_type=F32)
    y_b = jnp.dot(yb_ref[...], wpb_ref[...], preferred_element_type=F32)
    merged = _sigmoid(ga_ref[...].astype(F32)) * y_a + _sigmoid(gb_ref[...].astype(F32)) * y_b
    x1 = x_ref[...] + jnp.dot(merged.astype(BF16), wo_ref[...], preferred_element_type=F32)
    x1_ref[...] = x1
    ms = jnp.mean(x1 * x1, axis=-1, keepdims=True)
    xn = (x1 * lax.rsqrt(ms + RMS_EPS) * nf_ref[...]).astype(BF16)
    xn_ref[...] = xn
    logits = jnp.dot(xn, wr_ref[...], preferred_element_type=F32) + br_ref[...]
    lane = lax.broadcasted_iota(jnp.int32, logits.shape, 1)
    lane_f = lane.astype(F32)
    logits = jnp.where(lane < N_EXPERTS, logits, NEG_INF)
    vals, idxs = [], []
    for _ in range(TOP_K):
        mk = jnp.max(logits, axis=-1, keepdims=True)
        ik = jnp.min(jnp.where(logits == mk, lane_f, float(LANES)), axis=-1, keepdims=True)
        logits = jnp.where(lane_f == ik, NEG_INF, logits)
        vals.append(mk)
        idxs.append(ik)
    es = [jnp.exp(vk - vals[0]) for vk in vals]
    tot = es[0] + es[1] + es[2] + es[3]
    idx_out = jnp.zeros(logits.shape, F32)
    gate_out = jnp.zeros(logits.shape, F32)
    for kk in range(TOP_K):
        idx_out = jnp.where(lane == kk, idxs[kk], idx_out)
        gate_out = jnp.where(lane == kk, es[kk] / tot, gate_out)
    idx_ref[...] = jnp.transpose(idx_out)[0:SUBLANES, :].astype(jnp.int32)
    gate_ref[...] = gate_out


def _mix(os_, ls_, yb, z, x2d, p, n_all, row0, xn_buf=None):
    n, d = x2d.shape
    tm = min(512, n)
    assert row0 % tm == 0

    def rows(w, c=0):
        return pl.BlockSpec((tm, w), lambda i: (i, c))

    def full(a):
        return pl.BlockSpec(a.shape, lambda i: (0,) * a.ndim)

    grp = pl.BlockSpec((LANE_GROUPS, tm, LANES), lambda i: (0, i, 0))
    in_specs = [grp] * 6 + [rows(d), rows(d, 2), rows(d, 3), rows(d),
                            full(p["w_pa"]), full(p["w_pb"]), full(p["w_o"]), full(p["norm_ffn"]),
                            full(p["w_router"]), full(p["b_router"])]
    args = [*os_, *ls_, yb, z, z, x2d, p["w_pa"], p["w_pb"], p["w_o"], p["norm_ffn"], p["w_router"], p["b_router"]]
    aliases = {}
    if xn_buf is not None:
        aliases = {len(args): 1}
        in_specs.append(pl.BlockSpec(memory_space=pl.ANY))
        args.append(xn_buf)
    x1, xn, idx, gate = pl.pallas_call(
        _mix_kernel,
        grid=(n // tm,),
        in_specs=in_specs,
        out_specs=[rows(d), pl.BlockSpec((tm, d), lambda i: (i + row0 // tm, 0)),
                   pl.BlockSpec((SUBLANES, tm), lambda i: (0, i)), rows(LANES)],
        out_shape=[jax.ShapeDtypeStruct((n, d), F32), jax.ShapeDtypeStruct((n_all, d), BF16),
                   jax.ShapeDtypeStruct((SUBLANES, n), jnp.int32), jax.ShapeDtypeStruct((n, LANES), F32)],
        input_output_aliases=aliases,
        compiler_params=_params("parallel"),
        name="mix",
    )(*args)
    return x1, xn, idx, gate


def _expert_kernel(be_ref, nused_ref, x_ref, w1_ref, b1_ref, w2_ref, b2_ref, perm_ref, y_ref, w1_s, w2_s):
    i = pl.program_id(0)
    dff = w2_ref.shape[0]
    n_cb = w1_ref.shape[1] // MXU_DIM
    half = MXU_DIM // 2

    @pl.when((i == 0) | (be_ref[i] != be_ref[jnp.maximum(i - 1, 0)]))
    def _():
        for cb in range(n_cb):
            cs = slice(cb * MXU_DIM, (cb + 1) * MXU_DIM)
            w1_s[:, cs] = jnp.dot(w1_ref[:, cs].astype(BF16), perm_ref[...], preferred_element_type=F32).astype(BF16)
        w2_s[...] = w2_ref[...].astype(BF16)

    @pl.when(i < nused_ref[0])
    def _():
        h = jnp.dot(x_ref[...], w1_s[...], preferred_element_type=F32) + b1_ref[...]
        acts = []
        for cb in range(n_cb):
            x_glu = jnp.minimum(h[:, cb * MXU_DIM:cb * MXU_DIM + half], SWIGLU_LIMIT)
            x_lin = jnp.clip(h[:, cb * MXU_DIM + half:(cb + 1) * MXU_DIM], -SWIGLU_LIMIT, SWIGLU_LIMIT)
            acts.append(x_glu * _sigmoid(SWIGLU_ALPHA * x_glu) * (x_lin + 1.0))
        act = jnp.concatenate(acts, axis=-1)
        assert act.shape[-1] == dff
        y = jnp.dot(act.astype(BF16), w2_s[...], preferred_element_type=F32) + b2_ref[...]
        y_ref[...] = y.astype(y_ref.dtype)

    @pl.when(i >= nused_ref[0])
    def _():
        y_ref[...] = jnp.zeros_like(y_ref)


def _experts(x_rows, block_expert, n_used, p):
    n_rows, d = x_rows.shape
    n_blocks = n_rows // MOE_BLOCK
    dff2 = p["w1"].shape[-1]
    half = MXU_DIM // 2
    o = np.arange(MXU_DIM)
    perm = np.zeros((MXU_DIM, MXU_DIM), np.float32)
    perm[np.where(o < half, 2 * o, 2 * (o - half) + 1), o] = 1.0
    grid_spec = pltpu.PrefetchScalarGridSpec(
        num_scalar_prefetch=2,
        grid=(n_blocks,),
        in_specs=[
            pl.BlockSpec((MOE_BLOCK, d), lambda i, be, nu: (i, 0)),
            pl.BlockSpec((None, d, dff2), lambda i, be, nu: (be[i], 0, 0)),
            pl.BlockSpec((None, 1, dff2), lambda i, be, nu: (be[i], 0, 0)),
            pl.BlockSpec((None, dff2 // 2, d), lambda i, be, nu: (be[i], 0, 0)),
            pl.BlockSpec((None, 1, d), lambda i, be, nu: (be[i], 0, 0)),
            pl.BlockSpec((MXU_DIM, MXU_DIM), lambda i, be, nu: (0, 0)),
        ],
        out_specs=pl.BlockSpec((MOE_BLOCK, d), lambda i, be, nu: (i, 0)),
        scratch_shapes=[pltpu.VMEM((d, dff2), BF16), pltpu.VMEM((dff2 // 2, d), BF16)],
    )
    return pl.pallas_call(
        _expert_kernel,
        grid_spec=grid_spec,
        out_shape=jax.ShapeDtypeStruct((n_rows, d), BF16),
        compiler_params=_params("arbitrary"),
        name="experts",
    )(block_expert, n_used, x_rows, p["w1"], p["b1"], p["w2"], p["b2"], jnp.asarray(perm, BF16))


def _combine_kernel(y0, y1, y2, y3, gate_ref, x1_ref, nf_ref, o_ref):
    g = gate_ref[...]
    y = jnp.zeros(x1_ref.shape, F32)
    for kk, y_ref in enumerate((y0, y1, y2, y3)):
        y = y + y_ref[...].astype(F32) * g[:, kk:kk + 1]
    x = x1_ref[...] + y
    ms = jnp.mean(x * x, axis=-1, keepdims=True)
    o_ref[...] = x * lax.rsqrt(ms + RMS_EPS) * nf_ref[...]


def _combine(ys, gate, x1, norm_final):
    n, d = x1.shape
    tm = min(256, n)
    row = pl.BlockSpec((tm, d), lambda i: (i, 0))
    return pl.pallas_call(
        _combine_kernel,
        grid=(n // tm,),
        in_specs=[row] * TOP_K + [pl.BlockSpec((tm, LANES), lambda i: (i, 0)), row,
                                  pl.BlockSpec((1, d), lambda i: (0, 0))],
        out_specs=row,
        out_shape=jax.ShapeDtypeStruct((n, d), F32),
        compiler_params=_params("parallel"),
        name="combine",
    )(*ys, gate, x1, norm_final)


def _rank_kernel(idx_ref, tri_ref, rank_ref, cnt_ref, carry):
    @pl.when(pl.program_id(0) == 0)
    def _():
        carry[...] = jnp.zeros_like(carry)

    idx = idx_ref[...]
    tm = idx.shape[1]
    expert = lax.broadcasted_iota(jnp.int32, (N_EXPERTS, tm), 0)
    hits = [expert == idx[kk:kk + 1, :] for kk in range(TOP_K)]
    onehot = jnp.zeros((N_EXPERTS, tm), F32)
    for hit in hits:
        onehot = onehot + jnp.where(hit, 1.0, 0.0)
    before = jnp.dot(onehot.astype(BF16), tri_ref[...], preferred_element_type=F32) + carry[:, 0:1]
    row = lax.broadcasted_iota(jnp.int32, idx.shape, 0)
    out = jnp.zeros(idx.shape, F32)
    for kk, hit in enumerate(hits):
        out = jnp.where(row == kk, jnp.sum(jnp.where(hit, before, 0.0), axis=0, keepdims=True), out)
    rank_ref[...] = out.astype(jnp.int32)
    carry[...] = carry[...] + jnp.sum(onehot, axis=1, keepdims=True)
    cnt_ref[...] = carry[...].astype(jnp.int32)


def _rank(idx_t):
    n = idx_t.shape[1]
    tm = 512
    while n % tm:
        tm //= 2
    tri = jnp.asarray(np.triu(np.ones((tm, tm), np.float32), 1), BF16)
    return pl.pallas_call(
        _rank_kernel,
        grid=(n // tm,),
        in_specs=[pl.BlockSpec((SUBLANES, tm), lambda i: (0, i)), pl.BlockSpec((tm, tm), lambda i: (0, 0))],
        out_specs=[pl.BlockSpec((SUBLANES, tm), lambda i: (0, i)), pl.BlockSpec((N_EXPERTS, LANES), lambda i: (0, 0))],
        out_shape=[jax.ShapeDtypeStruct((SUBLANES, n), jnp.int32), jax.ShapeDtypeStruct((N_EXPERTS, LANES), jnp.int32)],
        scratch_shapes=[pltpu.VMEM((N_EXPERTS, LANES), F32)],
        compiler_params=_params("arbitrary"),
        name="rank",
    )(idx_t, tri)


def _route(idx_t):
    n_tok = idx_t.shape[1]
    n_assign = n_tok * TOP_K
    tok_bits = max(n_tok - 1, 1).bit_length()
    assert N_EXPERTS << tok_bits < 2 ** 31
    rank_t, cnt = _rank(idx_t)
    counts = cnt[:, 0]
    e_t = idx_t[:TOP_K]
    keys = jnp.sort(((e_t << tok_bits) + jnp.arange(n_tok, dtype=jnp.int32)[None, :]).reshape(-1))
    padded = (counts + MOE_BLOCK - 1) // MOE_BLOCK * MOE_BLOCK
    starts = jnp.cumsum(counts) - counts
    pends = jnp.cumsum(padded)
    pstarts = pends - padded
    experts = jnp.arange(N_EXPERTS, dtype=jnp.int32)[:, None, None]
    dest_t = jnp.sum(jnp.where(e_t[None] == experts, pstarts[:, None, None], 0), axis=0) + rank_t[:TOP_K]
    n_blocks = -(-(n_assign + N_EXPERTS * (MOE_BLOCK - 1)) // MOE_BLOCK)
    n_rows = n_blocks * MOE_BLOCK
    block_start = jnp.arange(n_blocks, dtype=jnp.int32) * MOE_BLOCK
    block_expert = jnp.minimum(jnp.sum((pends[None, :] <= block_start[:, None]).astype(jnp.int32), axis=1),
                               N_EXPERTS - 1)
    local = block_start - pstarts[block_expert]
    first = jnp.repeat(starts[block_expert] + local, MOE_BLOCK)
    live = jnp.repeat(counts[block_expert] - local, MOE_BLOCK)
    within = jnp.tile(jnp.arange(MOE_BLOCK, dtype=jnp.int32), n_blocks)
    src = keys[jnp.clip(first + within, 0, n_assign - 1)] & ((1 << tok_bits) - 1)
    row = jnp.arange(n_rows, dtype=jnp.int32)
    row_tok = jnp.where(within < live, src, row % n_tok)
    n_used = (pends[-1] // MOE_BLOCK).astype(jnp.int32).reshape(1)
    return row_tok, dest_t, block_expert, n_used


def _mixer(x, p, caches, conv0, h0, n_all, row0, xn_buf):
    batch, t, d = x.shape
    prompt = caches is None
    x2d = x.reshape(batch * t, d)
    os_, ls_, new_kv = [], [], []
    if prompt:
        nat = N_NATURAL_BLOCKS * GROUP_W
        z, zd = _in_proj_prompt(x2d, p["norm_mix"], p["w_in"], t, nat, ATT_DILATIONS[1:])
        zq = [z.reshape(batch, 1, t, nat)] + list(zd)
        for g in range(N_GROUPS):
            dil = ATT_DILATIONS[g]
            o, l = _attn_prompt(zq[g], g, COL_QKV if g == 0 else 0)
            keep = min(ATT_WINDOWS[g], t)
            if g == 0:
                kcol = (COL_QKV + 1) * GROUP_W
                last = z.reshape(batch, t, nat)[:, t - keep:, kcol:kcol + 2 * GROUP_W]
            else:
                last = zq[g][:, :, (t - keep) // dil:, GROUP_W:3 * GROUP_W]
                last = jnp.swapaxes(last, 1, 2).reshape(batch, keep, 2 * GROUP_W)
            kv = last.astype(F32).reshape(batch, keep, 2, HEADS_PER_GROUP, HEAD_DIM)
            os_.append(o)
            ls_.append(l)
            new_kv.append(kv[None])
        conv0 = jnp.zeros((batch, CONV_WIDTH - 1, d), F32)
        h0 = jnp.zeros((batch, d), F32)
    else:
        z = _in_proj(x2d, p["norm_mix"], p["w_in"], F32, 4)
        for g in range(N_GROUPS):
            o, l, kv = _attn_sample(z, caches[g], g)
            os_.append(o)
            ls_.append(l)
            new_kv.append(kv[None])
    yb, new_conv, h_last = _rglru(z.reshape(batch, t, -1), conv0, h0, p, prompt)
    x1, xn, idx, gate = _mix(os_, ls_, yb.reshape(batch * t, d), z, x2d, p, n_all, row0, xn_buf)
    return x1, xn, idx, gate, new_kv, new_conv[None], h_last[None]


def kernel(x_prompt, x_sample, cache_kv_g1, cache_kv_g2, cache_kv_g3, state_conv, state_h, norm_mix, w_in, w_pa,
           w_pb, w_o, conv_w, conv_b, lru_wa, lru_ba, lru_wi, lru_bi, lru_lambda, norm_ffn, w_router, b_router,
           w1, b1, w2, b2, norm_final):
    d = x_prompt.shape[-1]
    gsz = N_GROUPS * GROUP_W

    def block_diag(w):
        per = MXU_DIM // LRU_BLOCK_W
        w = w.reshape(-1, per, LRU_BLOCK_W, LRU_BLOCK_W)
        eye = jnp.eye(per, dtype=w.dtype)
        return jnp.einsum("gacd,ab->gacbd", w, eye).reshape(-1, MXU_DIM, MXU_DIM).astype(BF16)

    w_in0 = w_in[0]
    qkv = [w_in0[:, which * gsz + g * GROUP_W: which * gsz + (g + 1) * GROUP_W]
           for g in range(N_GROUPS) for which in range(3)]
    n_e, dff2 = b1.shape[1], b1.shape[2]
    half = MXU_DIM // 2
    p = dict(
        norm_mix=norm_mix[0][None], norm_ffn=norm_ffn[0][None],
        w_in=jnp.concatenate([w_in0[:, 3 * gsz:]] + qkv, axis=1).astype(BF16),
        w_pa=w_pa[0].astype(BF16), w_pb=w_pb[0].astype(BF16), w_o=w_o[0].astype(BF16),
        conv_w=conv_w[0], conv_b=conv_b[0][None],
        wa=block_diag(lru_wa[0]), wi=block_diag(lru_wi[0]),
        ba=lru_ba[0].reshape(1, d), bi=lru_bi[0].reshape(1, d), lam=lru_lambda[0][None],
        w_router=jnp.pad(w_router[0], ((0, 0), (0, LANES - N_EXPERTS))).astype(BF16),
        b_router=jnp.pad(b_router[0], (0, LANES - N_EXPERTS))[None],
        w1=w1[0], w2=w2[0],
        b1=b1[0].reshape(n_e, dff2 // MXU_DIM, half, 2).transpose(0, 1, 3, 2).reshape(n_e, 1, dff2),
        b2=b2[0][:, None, :],
    )

    n_p = x_prompt.shape[0] * x_prompt.shape[1]
    n_all = n_p + x_sample.shape[0] * x_sample.shape[1]
    x1_p, xn_all, idx_p, gate_p, kv_p, conv_p, h_p = _mixer(x_prompt, p, None, None, None, n_all, 0, None)
    x1_s, xn_all, idx_s, gate_s, kv_s, conv_s, h_s = _mixer(
        x_sample, p, [cache_kv_g1[0], cache_kv_g2[0], cache_kv_g3[0]], state_conv[0], state_h[0], n_all, n_p, xn_all)

    row_tok, dest, block_expert, n_used = _route(jnp.concatenate([idx_p, idx_s], axis=1))
    y_rows = _experts(xn_all[row_tok], block_expert, n_used, p)
    nf = norm_final[None]
    y_p = _combine([y_rows[dest[kk, :n_p]] for kk in range(TOP_K)], gate_p, x1_p, nf).reshape(x_prompt.shape)
    y_s = _combine([y_rows[dest[kk, n_p:]] for kk in range(TOP_K)], gate_s, x1_s, nf).reshape(x_sample.shape)
    return (y_p, y_s, kv_p[0], kv_p[1], kv_p[2], conv_p, h_p, kv_s[0], kv_s[1], kv_s[2], conv_s, h_s)
```

```python
import functools

import numpy as np
import jax
import jax.numpy as jnp
from jax import lax
from jax.experimental import pallas as pl
from jax.experimental.pallas import tpu as pltpu

F32 = jnp.float32
BF16 = jnp.bfloat16

N_GROUPS = 3
HEADS_PER_GROUP = 8
HEAD_DIM = 64
GROUP_W = HEADS_PER_GROUP * HEAD_DIM
ATT_WINDOWS = (128, 512, 2048)
ATT_DILATIONS = (1, 4, 16)
ATT_BLOCK = 128
N_ATT_HEADS = N_GROUPS * HEADS_PER_GROUP
CONV_WIDTH = 4
LRU_C = 8.0
LRU_BLOCK_W = 64
N_EXPERTS = 32
TOP_K = 4
SWIGLU_ALPHA = 1.702
SWIGLU_LIMIT = 7.0
RMS_EPS = 1e-6
NEG_INF = float("-inf")

LANES = 128
SUBLANES = 8
MXU_DIM = 256
VMEM_LIMIT_BYTES = 56 * 1024 * 1024

COL_QKV = 8
N_NATURAL_BLOCKS = COL_QKV + 3
LANE_GROUPS = GROUP_W // LANES
MOE_BLOCK = 512
MOE_BLOCK_SMALL = 128
PROJ_TILE = 512
ATT_UNITS = 2

_SLOPES = [float(np.float32(2.0 ** (-8.0 * (i + 1) / N_ATT_HEADS))) for i in range(N_ATT_HEADS)]


def _params(*sem):
    return pltpu.CompilerParams(dimension_semantics=sem, vmem_limit_bytes=VMEM_LIMIT_BYTES)


def _sigmoid(x):
    return 0.5 * jnp.tanh(0.5 * x) + 0.5


def _log2(n):
    assert n > 0 and n & (n - 1) == 0, n
    return n.bit_length() - 1


def _rmsnorm_bf16(x, gain):
    ms = jnp.mean(x * x, axis=-1, keepdims=True)
    return (x * lax.rsqrt(ms + RMS_EPS) * gain).astype(BF16)


def _in_proj_kernel(x_ref, g_ref, w_ref, o_ref, xn_ref):
    @pl.when(pl.program_id(1) == 0)
    def _():
        xn_ref[...] = _rmsnorm_bf16(x_ref[...], g_ref[...])

    o_ref[...] = jnp.dot(xn_ref[...], w_ref[...], preferred_element_type=F32).astype(o_ref.dtype)


def _in_proj(x2d, gain, w, out_dtype, n_col_tiles):
    n, d = x2d.shape
    cols = w.shape[1]
    tm = min(PROJ_TILE, n)
    tn = cols // n_col_tiles
    return pl.pallas_call(
        _in_proj_kernel,
        grid=(n // tm, n_col_tiles),
        in_specs=[
            pl.BlockSpec((tm, d), lambda i, j: (i, 0)),
            pl.BlockSpec((1, d), lambda i, j: (0, 0)),
            pl.BlockSpec((d, tn), lambda i, j: (0, j)),
        ],
        out_specs=pl.BlockSpec((tm, tn), lambda i, j: (i, j)),
        out_shape=jax.ShapeDtypeStruct((n, cols), out_dtype),
        scratch_shapes=[pltpu.VMEM((tm, d), BF16)],
        compiler_params=_params("parallel", "arbitrary"),
        name="in_proj",
    )(x2d, gain, w)


def _in_proj_prompt_kernel(x_ref, g_ref, w_ref, *refs, dilations, n_chunks):
    n_dil = len(dilations)
    perm_refs, zn_ref, zd_refs = refs[:n_dil], refs[n_dil], refs[n_dil + 1:]
    xn = _rmsnorm_bf16(x_ref[...], g_ref[...])
    nat = zn_ref.shape[1]
    cw = nat // n_chunks
    for c in range(n_chunks):
        cs = slice(c * cw, (c + 1) * cw)
        zn_ref[:, cs] = jnp.dot(xn, w_ref[:, cs], preferred_element_type=F32).astype(zn_ref.dtype)
    col = nat
    for perm_ref, zd_ref, dil in zip(perm_refs, zd_refs, dilations):
        width = zd_ref.shape[-1]
        xp = jnp.dot(perm_ref[...], xn, preferred_element_type=F32).astype(BF16)
        res = jnp.dot(xp, w_ref[:, col:col + width], preferred_element_type=F32).astype(zd_ref.dtype)
        per = res.shape[0] // dil
        for r in range(dil):
            zd_ref[r] = res[r * per:(r + 1) * per, :]
        col += width


def _in_proj_prompt(x2d, gain, w, seq, nat, dilations):
    n, d = x2d.shape
    cols = w.shape[1]
    tm = PROJ_TILE // 2
    width = 3 * GROUP_W
    assert nat + width * len(dilations) == cols and seq % tm == 0
    tiles_per_seq = seq // tm
    perms = []
    for dil in dilations:
        per = tm // dil
        o = np.arange(tm)
        perm = np.zeros((tm, tm), np.float32)
        perm[o, (o % per) * dil + o // per] = 1.0
        perms.append(jnp.asarray(perm, BF16))
    const = lambda shape: pl.BlockSpec(shape, lambda i: (0,) * len(shape))
    outs = pl.pallas_call(
        functools.partial(_in_proj_prompt_kernel, dilations=tuple(dilations), n_chunks=2),
        grid=(n // tm,),
        in_specs=[pl.BlockSpec((tm, d), lambda i: (i, 0)), const((1, d)),
                  pl.BlockSpec((d, cols), lambda i: (0, 0), pipeline_mode=pl.Buffered(1))]
        + [const((tm, tm))] * len(dilations),
        out_specs=[pl.BlockSpec((tm, nat), lambda i: (i, 0))]
        + [pl.BlockSpec((None, dil, tm // dil, width), lambda i: (i // tiles_per_seq, 0, i % tiles_per_seq, 0))
           for dil in dilations],
        out_shape=[jax.ShapeDtypeStruct((n, nat), BF16)]
        + [jax.ShapeDtypeStruct((n // seq, dil, seq // dil, width), BF16) for dil in dilations],
        compiler_params=_params("parallel"),
        name="in_proj_prompt",
    )(x2d, gain, w, *perms)
    return outs[0], outs[1:]


def _attn_prompt_kernel(q_ref, kp_ref, kc_ref, vp_ref, vc_ref, b_ref, o_ref, l_ref, *, dilation, n_sub, units):
    blk = ATT_BLOCK
    j = pl.program_id(1)
    low = lax.broadcasted_iota(jnp.int32, (blk, LANES), 1) < HEAD_DIM

    def unit(r, u):
        own = slice(u * blk, (u + 1) * blk)
        q = q_ref[r, own, :] * (HEAD_DIM ** -0.5)
        if u == 0:
            k = jnp.concatenate([kp_ref[r], kc_ref[r, own, :]], axis=0)
            v = jnp.concatenate([vp_ref[r], vc_ref[r, own, :]], axis=0)
            bias = b_ref.at[jnp.minimum(j, 1)]
        else:
            both = slice((u - 1) * blk, (u + 1) * blk)
            k, v = kc_ref[r, both, :], vc_ref[r, both, :]
            bias = b_ref.at[1]
        rows = pl.ds(u * blk * dilation + r, blk, stride=dilation) if dilation > 1 else own
        for p in range(LANE_GROUPS):
            cs = slice(p * LANES, (p + 1) * LANES)
            qg, kg, vg = q[:, cs], k[:, cs], v[:, cs]
            outs, lses = [], []
            for hh in range(2):
                sel = low if hh == 0 else jnp.logical_not(low)
                qm = jnp.where(sel, qg, jnp.zeros_like(qg))
                s = lax.dot_general(qm, kg, (((1,), (1,)), ((), ())), preferred_element_type=F32)
                s = s + bias[2 * p + hh]
                m = jnp.max(s, axis=-1, keepdims=True)
                e = jnp.exp(s - m)
                den = jnp.sum(e, axis=-1, keepdims=True)
                outs.append(jnp.dot(e.astype(BF16), vg, preferred_element_type=F32) / den)
                lses.append(m + jnp.log(den))
            o_ref[p, rows, :] = jnp.where(low, outs[0], outs[1])
            l_ref[p, rows, :] = jnp.where(low, lses[0], lses[1])

    per_r = min(units, n_sub)
    r_per_body = units // per_r
    assert n_sub % per_r == 0 and dilation % r_per_body == 0

    def body(i, carry):
        for rr in range(r_per_body):
            for u in range(n_sub):
                unit(i * r_per_body + rr, u)
        return carry

    if dilation == r_per_body:
        body(0, 0)
    else:
        lax.fori_loop(0, dilation // r_per_body, body, 0)


def _band_bias(g):
    blk, d = ATT_BLOCK, ATT_DILATIONS[g]
    qq = np.arange(blk)[:, None]
    kk = np.arange(2 * blk)[None, :]
    step = qq + blk - kk
    band = (step >= 0) & (step <= blk)
    slopes = np.asarray(_SLOPES[g * HEADS_PER_GROUP:(g + 1) * HEADS_PER_GROUP], np.float32)
    bias = -slopes[:, None, None] * (step * d).astype(np.float32)[None]
    out = np.empty((2, HEADS_PER_GROUP, blk, 2 * blk), np.float32)
    out[0] = np.where(band & (kk >= blk), bias, -np.inf)
    out[1] = np.where(band, bias, -np.inf)
    return jnp.asarray(out)


def _attn_prompt(zq, g, col0):
    d = ATT_DILATIONS[g]
    batch, _, n, _ = zq.shape
    n_sub = max(1, ATT_UNITS // d)
    nb = n // (ATT_BLOCK * n_sub)
    span = ATT_BLOCK * n_sub

    def cur(c):
        return pl.BlockSpec((None, d, span, GROUP_W), lambda b, j: (b, 0, j, c))

    def prev(c):
        return pl.BlockSpec((None, d, ATT_BLOCK, GROUP_W), lambda b, j: (b, 0, jnp.maximum(j * n_sub - 1, 0), c))

    out_spec = pl.BlockSpec((LANE_GROUPS, span * d, LANES), lambda b, j: (0, b * nb + j, 0))
    out_sds = jax.ShapeDtypeStruct((LANE_GROUPS, batch * n * d, LANES), F32)
    bias = _band_bias(g)
    return pl.pallas_call(
        functools.partial(_attn_prompt_kernel, dilation=d, n_sub=n_sub, units=ATT_UNITS),
        grid=(batch, nb),
        in_specs=[cur(col0), prev(col0 + 1), cur(col0 + 1), prev(col0 + 2), cur(col0 + 2),
                  pl.BlockSpec(bias.shape, lambda b, j: (0, 0, 0, 0))],
        out_specs=[out_spec, out_spec],
        out_shape=[out_sds, out_sds],
        compiler_params=_params("parallel", "arbitrary"),
        name=f"attn_prompt_g{g}",
    )(zq, zq, zq, zq, zq, bias)


def _attn_sample_kernel(q_ref, k_ref, v_ref, c_ref, sl_ref, o_ref, l_ref, cout_ref, *, window, dilation):
    t_new = q_ref.shape[0]
    rows = HEADS_PER_GROUP * t_new
    q = q_ref[...] * (HEAD_DIM ** -0.5)
    pad = jnp.zeros((LANES - t_new, GROUP_W), F32)
    kn = jnp.concatenate([k_ref[...], pad], axis=0)
    vn = jnp.concatenate([v_ref[...], pad], axis=0)
    new_t = jnp.concatenate([kn.T, vn.T], axis=0)
    cin = c_ref[...]
    cout_ref[...] = pltpu.roll(cin, window - t_new, axis=1)
    tail = cout_ref[:, window - LANES:window]
    lane = lax.broadcasted_iota(jnp.int32, tail.shape, 1)
    cout_ref[:, window - LANES:window] = jnp.where(lane >= LANES - t_new,
                                                   pltpu.roll(new_t, LANES - t_new, axis=1), tail)
    qt = jnp.concatenate([q] * HEADS_PER_GROUP, axis=0)
    row_h = lax.broadcasted_iota(jnp.int32, (rows, GROUP_W), 0) >> _log2(t_new)
    col_h = lax.broadcasted_iota(jnp.int32, (rows, GROUP_W), 1) >> _log2(HEAD_DIM)
    qbd = jnp.where(row_h == col_h, qt, 0.0).astype(BF16)
    k_t = cin[0:GROUP_W, :].astype(BF16)
    v_t = cin[GROUP_W:2 * GROUP_W, :].astype(BF16)
    s_c = jnp.dot(qbd, k_t, preferred_element_type=F32)
    s_n = jnp.dot(qbd, new_t[0:GROUP_W, :].astype(BF16), preferred_element_type=F32)
    _log2(dilation)
    slope = sl_ref[:, 0:1]
    t_c = lax.broadcasted_iota(jnp.int32, s_c.shape, 0) & (t_new - 1)
    dist_c = window + t_c - lax.broadcasted_iota(jnp.int32, s_c.shape, 1)
    s_c = jnp.where((dist_c <= window) & ((dist_c & (dilation - 1)) == 0), s_c - slope * dist_c.astype(F32), NEG_INF)
    t_n = lax.broadcasted_iota(jnp.int32, s_n.shape, 0) & (t_new - 1)
    dist_n = t_n - lax.broadcasted_iota(jnp.int32, s_n.shape, 1)
    s_n = jnp.where((dist_n >= 0) & ((dist_n & (dilation - 1)) == 0), s_n - slope * dist_n.astype(F32), NEG_INF)
    m = jnp.maximum(jnp.max(s_c, axis=-1, keepdims=True), jnp.max(s_n, axis=-1, keepdims=True))
    e_c = jnp.exp(s_c - m)
    e_n = jnp.exp(s_n - m)
    den = jnp.sum(e_c, axis=-1, keepdims=True) + jnp.sum(e_n, axis=-1, keepdims=True)
    o_full = lax.dot_general(e_c.astype(BF16), v_t, (((1,), (1,)), ((), ())), preferred_element_type=F32)
    o_full = (o_full + jnp.dot(e_n.astype(BF16), vn.astype(BF16), preferred_element_type=F32)) / den
    lse = m + jnp.log(den)
    out_h = lax.broadcasted_iota(jnp.int32, (t_new, GROUP_W), 1) >> _log2(HEAD_DIM)
    o = jnp.zeros((t_new, GROUP_W), F32)
    l = jnp.zeros((t_new, GROUP_W), F32)
    for h in range(HEADS_PER_GROUP):
        rs = slice(h * t_new, (h + 1) * t_new)
        o = jnp.where(out_h == h, o_full[rs, :], o)
        l = jnp.where(out_h == h, lse[rs, :], l)
    for c in range(LANE_GROUPS):
        o_ref[c] = o[:, c * LANES:(c + 1) * LANES]
        l_ref[c] = l[:, c * LANES:(c + 1) * LANES]


def _attn_sample(z, cache, g):
    batch, window = cache.shape[0], cache.shape[1]
    n_tok = z.shape[0]
    t_new = n_tok // batch
    d = ATT_DILATIONS[g]
    c_t = jnp.transpose(cache, (0, 2, 3, 4, 1)).reshape(batch, 2 * GROUP_W, window)
    slopes = np.repeat(np.asarray(_SLOPES[g * HEADS_PER_GROUP:(g + 1) * HEADS_PER_GROUP], np.float32), t_new)
    slopes = jnp.asarray(np.broadcast_to(slopes[:, None], (HEADS_PER_GROUP * t_new, LANES)))
    col0 = COL_QKV + 3 * g

    def col(c):
        return pl.BlockSpec((t_new, GROUP_W), lambda b: (b, c))

    tok_spec = pl.BlockSpec((LANE_GROUPS, t_new, LANES), lambda b: (0, b, 0))
    tok_sds = jax.ShapeDtypeStruct((LANE_GROUPS, n_tok, LANES), F32)
    buf_spec = pl.BlockSpec((None, 2 * GROUP_W, window), lambda b: (b, 0, 0))
    o, l, cout = pl.pallas_call(
        functools.partial(_attn_sample_kernel, window=window, dilation=d),
        grid=(batch,),
        in_specs=[col(col0), col(col0 + 1), col(col0 + 2), buf_spec, pl.BlockSpec(slopes.shape, lambda b: (0, 0))],
        out_specs=[tok_spec, tok_spec, buf_spec],
        out_shape=[tok_sds, tok_sds, jax.ShapeDtypeStruct(c_t.shape, F32)],
        compiler_params=_params("parallel"),
        name=f"attn_sample_g{g}",
    )(z, z, z, c_t, slopes)
    cout = cout.reshape(batch, 2, HEADS_PER_GROUP, HEAD_DIM, window)
    return o, l, jnp.transpose(cout, (0, 4, 1, 2, 3))


def _gelu_tanh(x):
    c1 = float(np.sqrt(2.0 / np.pi))
    return x * (0.5 * jnp.tanh(x * (c1 + (c1 * 0.044715) * (x * x))) + 0.5)


def _rglru_kernel(xb_ref, yg_ref, c0_ref, h0_ref, cw_ref, cb_ref, wa_ref, wi_ref, ba_ref, bi_ref, lam_ref,
                  yb_ref, cout_ref, hout_ref, xpad, a_s, u_s, h_s, *, tc, seg, reset_first):
    nb = xb_ref.shape[0]
    ngl = a_s.shape[0]
    j = pl.program_id(1)

    @pl.when(j == 0)
    def _():
        xpad[:, 0:SUBLANES, :] = c0_ref[...]
        h_s[...] = h0_ref[...]

    @pl.when(j > 0)
    def _():
        xpad[:, 0:SUBLANES, :] = xpad[:, tc:tc + SUBLANES, :]

    xpad[:, SUBLANES:SUBLANES + tc, :] = xb_ref[...].astype(F32)
    cout_ref[...] = xpad[:, tc:tc + SUBLANES, :]

    lam = lam_ref[...]
    softplus_neg = jnp.maximum(-lam, 0.0) + jnp.log1p(jnp.exp(-jnp.abs(lam)))
    first = (lax.broadcasted_iota(jnp.int32, (tc, 1), 0) == 0) & (j == 0)
    gw = wa_ref.shape[1]
    for b in range(nb):
        xc = cb_ref[...]
        for tap in range(CONV_WIDTH):
            off = SUBLANES - (CONV_WIDTH - 1) + tap
            xc = xc + xpad[b, off:off + tc, :] * cw_ref[tap:tap + 1, :]
        xcb = xc.astype(BF16)
        ra, ri = [], []
        for blk in range(wa_ref.shape[0]):
            xs = xcb[:, blk * gw:(blk + 1) * gw]
            ra.append(jnp.dot(xs, wa_ref[blk], preferred_element_type=F32))
            ri.append(jnp.dot(xs, wi_ref[blk], preferred_element_type=F32))
        r = _sigmoid(jnp.concatenate(ra, axis=-1) + ba_ref[...])
        i = _sigmoid(jnp.concatenate(ri, axis=-1) + bi_ref[...])
        log_a = -LRU_C * r * softplus_neg
        a = jnp.exp(log_a)
        w = 1.0 - a * a
        mult = jnp.where(w > 0.0, w * lax.rsqrt(w), 0.0)
        if reset_first:
            mult = jnp.where(first, 1.0, mult)
        u = mult * (i * xc)
        for c in range(ngl):
            a_s[c, b * seg:b * seg + tc, :] = a[:, c * LANES:(c + 1) * LANES]
            u_s[c, b * seg:b * seg + tc, :] = u[:, c * LANES:(c + 1) * LANES]

    def step(t, hs):
        idx = pl.ds(t, nb, stride=seg)
        out = []
        for c in range(ngl):
            h = a_s[c, idx, :] * hs[c] + u_s[c, idx, :]
            u_s[c, idx, :] = h
            out.append(h)
        return tuple(out)

    hs = lax.fori_loop(0, tc, step, tuple(h_s[:, c * LANES:(c + 1) * LANES] for c in range(ngl)))
    h = jnp.concatenate(hs, axis=-1)
    h_s[...] = h
    hout_ref[...] = h
    for b in range(nb):
        hb = jnp.concatenate([u_s[c, b * seg:b * seg + tc, :] for c in range(ngl)], axis=-1)
        yb_ref[b] = (_gelu_tanh(yg_ref[b].astype(F32)) * hb).astype(yb_ref.dtype)


def _rglru(z3, conv0, h0, p, reset_first):
    batch, t, _ = z3.shape
    width = h0.shape[-1]
    nb = SUBLANES
    tc = min(128, t)
    seg = tc + SUBLANES
    c0 = jnp.pad(conv0, ((0, 0), (SUBLANES - (CONV_WIDTH - 1), 0), (0, 0)))
    cw = jnp.pad(p["conv_w"], ((0, SUBLANES - CONV_WIDTH), (0, 0)))
    ngrp = p["wa"].shape[0]
    gw = p["wa"].shape[1]

    def full(shape):
        return pl.BlockSpec(shape, lambda i, j: (0,) * len(shape))

    yb, cout, hout = pl.pallas_call(
        functools.partial(_rglru_kernel, tc=tc, seg=seg, reset_first=reset_first),
        grid=(batch // nb, t // tc),
        in_specs=[
            pl.BlockSpec((nb, tc, width), lambda i, j: (i, j, 0)),
            pl.BlockSpec((nb, tc, width), lambda i, j: (i, j, 1)),
            pl.BlockSpec((nb, SUBLANES, width), lambda i, j: (i, 0, 0)),
            pl.BlockSpec((nb, width), lambda i, j: (i, 0)),
            full((SUBLANES, width)), full((1, width)),
            full((ngrp, gw, gw)), full((ngrp, gw, gw)),
            full((1, width)), full((1, width)), full((1, width)),
        ],
        out_specs=[
            pl.BlockSpec((nb, tc, width), lambda i, j: (i, j, 0)),
            pl.BlockSpec((nb, SUBLANES, width), lambda i, j: (i, 0, 0)),
            pl.BlockSpec((nb, width), lambda i, j: (i, 0)),
        ],
        out_shape=[
            jax.ShapeDtypeStruct((batch, t, width), BF16),
            jax.ShapeDtypeStruct((batch, SUBLANES, width), F32),
            jax.ShapeDtypeStruct((batch, width), F32),
        ],
        scratch_shapes=[
            pltpu.VMEM((nb, tc + SUBLANES, width), F32),
            pltpu.VMEM((width // LANES, nb * seg, LANES), F32),
            pltpu.VMEM((width // LANES, nb * seg, LANES), F32),
            pltpu.VMEM((nb, width), F32),
        ],
        compiler_params=_params("parallel", "arbitrary"),
        name="rglru",
    )(z3, z3, c0, h0, cw, p["conv_b"], p["wa"], p["wi"], p["ba"], p["bi"], p["lam"])
    return yb, cout[:, SUBLANES - (CONV_WIDTH - 1):, :], hout


def _mix_kernel(o1, o2, o3, l1, l2, l3, yb_ref, ga_ref, gb_ref, x_ref, wpa_ref, wpb_ref, wo_ref, nf_ref,
                wr_ref, br_ref, x1_ref, xn_ref, idx_ref, gate_ref):
    parts = []
    for c in range(LANE_GROUPS):
        la, lb, lc = l1[c], l2[c], l3[c]
        m = jnp.maximum(jnp.maximum(la, lb), lc)
        ea, eb, ec = jnp.exp(la - m), jnp.exp(lb - m), jnp.exp(lc - m)
        inv = 1.0 / (ea + eb + ec)
        parts.append((ea * inv) * o1[c] + (eb * inv) * o2[c] + (ec * inv) * o3[c])
    att = jnp.concatenate(parts, axis=-1)
    y_a = jnp.dot(att.astype(BF16), wpa_ref[...], preferred_element_type=F32)
    y_b = jnp.dot(yb_ref[...], wpb_ref[...], preferred_element_type=F32)
    merged = _sigmoid(ga_ref[...].astype(F32)) * y_a + _sigmoid(gb_ref[...].astype(F32)) * y_b
    x1 = x_ref[...] + jnp.dot(merged.astype(BF16), wo_ref[...], preferred_element_type=F32)
    x1_ref[...] = x1
    ms = jnp.mean(x1 * x1, axis=-1, keepdims=True)
    xn = (x1 * lax.rsqrt(ms + RMS_EPS) * nf_ref[...]).astype(BF16)
    xn_ref[...] = xn
    logits = jnp.dot(xn, wr_ref[...], preferred_element_type=F32) + br_ref[...]
    lane = lax.broadcasted_iota(jnp.int32, logits.shape, 1)
    lane_f = lane.astype(F32)
    logits = jnp.where(lane < N_EXPERTS, logits, NEG_INF)
    vals, idxs = [], []
    for _ in range(TOP_K):
        mk = jnp.max(logits, axis=-1, keepdims=True)
        ik = jnp.min(jnp.where(logits == mk, lane_f, float(LANES)), axis=-1, keepdims=True)
        logits = jnp.where(lane_f == ik, NEG_INF, logits)
        vals.append(mk)
        idxs.append(ik)
    es = [jnp.exp(vk - vals[0]) for vk in vals]
    tot = es[0] + es[1] + es[2] + es[3]
    idx_out = jnp.zeros(logits.shape, F32)
    gate_out = jnp.zeros(logits.shape, F32)
    for kk in range(TOP_K):
        idx_out = jnp.where(lane == kk, idxs[kk], idx_out)
        gate_out = jnp.where(lane == kk, es[kk] / tot, gate_out)
    idx_ref[...] = jnp.transpose(idx_out)[0:SUBLANES, :].astype(jnp.int32)
    gate_ref[...] = gate_out


def _mix(os_, ls_, yb, z, x2d, p):
    n, d = x2d.shape
    tm = min(512, n)

    def rows(w, c=0):
        return pl.BlockSpec((tm, w), lambda i: (i, c))

    def full(a):
        return pl.BlockSpec(a.shape, lambda i: (0,) * a.ndim)

    grp = pl.BlockSpec((LANE_GROUPS, tm, LANES), lambda i: (0, i, 0))
    in_specs = [grp] * 6 + [rows(d), rows(d, 2), rows(d, 3), rows(d),
                            full(p["w_pa"]), full(p["w_pb"]), full(p["w_o"]), full(p["norm_ffn"]),
                            full(p["w_router"]), full(p["b_router"])]
    args = [*os_, *ls_, yb, z, z, x2d, p["w_pa"], p["w_pb"], p["w_o"], p["norm_ffn"], p["w_router"], p["b_router"]]
    x1, xn, idx, gate = pl.pallas_call(
        _mix_kernel,
        grid=(n // tm,),
        in_specs=in_specs,
        out_specs=[rows(d), rows(d),
                   pl.BlockSpec((SUBLANES, tm), lambda i: (0, i)), rows(LANES)],
        out_shape=[jax.ShapeDtypeStruct((n, d), F32), jax.ShapeDtypeStruct((n, d), BF16),
                   jax.ShapeDtypeStruct((SUBLANES, n), jnp.int32), jax.ShapeDtypeStruct((n, LANES), F32)],
        compiler_params=_params("parallel"),
        name="mix",
    )(*args)
    return x1, xn, idx, gate


def _expert_kernel(be_ref, nused_ref, x_ref, w1_ref, b1_ref, w2_ref, b2_ref, perm_ref, y_ref, w1_s, w2_s):
    i = pl.program_id(0)
    dff = w2_ref.shape[0]
    n_cb = w1_ref.shape[1] // MXU_DIM
    half = MXU_DIM // 2

    @pl.when((i == 0) | (be_ref[i] != be_ref[jnp.maximum(i - 1, 0)]))
    def _():
        for cb in range(n_cb):
            cs = slice(cb * MXU_DIM, (cb + 1) * MXU_DIM)
            w1_s[:, cs] = jnp.dot(w1_ref[:, cs].astype(BF16), perm_ref[...], preferred_element_type=F32).astype(BF16)
        w2_s[...] = w2_ref[...].astype(BF16)

    @pl.when(i < nused_ref[0])
    def _():
        h = jnp.dot(x_ref[...], w1_s[...], preferred_element_type=F32) + b1_ref[...]
        acts = []
        for cb in range(n_cb):
            x_glu = jnp.minimum(h[:, cb * MXU_DIM:cb * MXU_DIM + half], SWIGLU_LIMIT)
            x_lin = jnp.clip(h[:, cb * MXU_DIM + half:(cb + 1) * MXU_DIM], -SWIGLU_LIMIT, SWIGLU_LIMIT)
            acts.append(x_glu * _sigmoid(SWIGLU_ALPHA * x_glu) * (x_lin + 1.0))
        act = jnp.concatenate(acts, axis=-1)
        assert act.shape[-1] == dff
        y = jnp.dot(act.astype(BF16), w2_s[...], preferred_element_type=F32) + b2_ref[...]
        y_ref[...] = y.astype(y_ref.dtype)

    @pl.when(i >= nused_ref[0])
    def _():
        y_ref[...] = jnp.zeros_like(y_ref)


def _experts(x_rows, block_expert, n_used, p, block):
    n_rows, d = x_rows.shape
    n_blocks = n_rows // block
    dff2 = p["w1"].shape[-1]
    half = MXU_DIM // 2
    o = np.arange(MXU_DIM)
    perm = np.zeros((MXU_DIM, MXU_DIM), np.float32)
    perm[np.where(o < half, 2 * o, 2 * (o - half) + 1), o] = 1.0
    grid_spec = pltpu.PrefetchScalarGridSpec(
        num_scalar_prefetch=2,
        grid=(n_blocks,),
        in_specs=[
            pl.BlockSpec((block, d), lambda i, be, nu: (i, 0)),
            pl.BlockSpec((None, d, dff2), lambda i, be, nu: (be[i], 0, 0)),
            pl.BlockSpec((None, 1, dff2), lambda i, be, nu: (be[i], 0, 0)),
            pl.BlockSpec((None, dff2 // 2, d), lambda i, be, nu: (be[i], 0, 0)),
            pl.BlockSpec((None, 1, d), lambda i, be, nu: (be[i], 0, 0)),
            pl.BlockSpec((MXU_DIM, MXU_DIM), lambda i, be, nu: (0, 0)),
        ],
        out_specs=pl.BlockSpec((block, d), lambda i, be, nu: (i, 0)),
        scratch_shapes=[pltpu.VMEM((d, dff2), BF16), pltpu.VMEM((dff2 // 2, d), BF16)],
    )
    return pl.pallas_call(
        _expert_kernel,
        grid_spec=grid_spec,
        out_shape=jax.ShapeDtypeStruct((n_rows, d), BF16),
        compiler_params=_params("arbitrary"),
        name="experts",
    )(block_expert, n_used, x_rows, p["w1"], p["b1"], p["w2"], p["b2"], jnp.asarray(perm, BF16))


def _combine_kernel(y0, y1, y2, y3, gate_ref, x1_ref, nf_ref, o_ref):
    g = gate_ref[...]
    y = jnp.zeros(x1_ref.shape, F32)
    for kk, y_ref in enumerate((y0, y1, y2, y3)):
        y = y + y_ref[...].astype(F32) * g[:, kk:kk + 1]
    x = x1_ref[...] + y
    ms = jnp.mean(x * x, axis=-1, keepdims=True)
    o_ref[...] = x * lax.rsqrt(ms + RMS_EPS) * nf_ref[...]


def _combine(ys, gate, x1, norm_final):
    n, d = x1.shape
    tm = min(256, n)
    row = pl.BlockSpec((tm, d), lambda i: (i, 0))
    return pl.pallas_call(
        _combine_kernel,
        grid=(n // tm,),
        in_specs=[row] * TOP_K + [pl.BlockSpec((tm, LANES), lambda i: (i, 0)), row,
                                  pl.BlockSpec((1, d), lambda i: (0, 0))],
        out_specs=row,
        out_shape=jax.ShapeDtypeStruct((n, d), F32),
        compiler_params=_params("parallel"),
        name="combine",
    )(*ys, gate, x1, norm_final)


def _rank_kernel(idx_ref, tri_ref, rank_ref, cnt_ref, carry):
    @pl.when(pl.program_id(0) == 0)
    def _():
        carry[...] = jnp.zeros_like(carry)

    idx = idx_ref[...]
    tm = idx.shape[1]
    expert = lax.broadcasted_iota(jnp.int32, (N_EXPERTS, tm), 0)
    hits = [expert == idx[kk:kk + 1, :] for kk in range(TOP_K)]
    onehot = jnp.zeros((N_EXPERTS, tm), F32)
    for hit in hits:
        onehot = onehot + jnp.where(hit, 1.0, 0.0)
    before = jnp.dot(onehot.astype(BF16), tri_ref[...], preferred_element_type=F32) + carry[:, 0:1]
    row = lax.broadcasted_iota(jnp.int32, idx.shape, 0)
    out = jnp.zeros(idx.shape, F32)
    for kk, hit in enumerate(hits):
        out = jnp.where(row == kk, jnp.sum(jnp.where(hit, before, 0.0), axis=0, keepdims=True), out)
    rank_ref[...] = out.astype(jnp.int32)
    carry[...] = carry[...] + jnp.sum(onehot, axis=1, keepdims=True)
    cnt_ref[...] = carry[...].astype(jnp.int32)


def _rank(idx_t):
    n = idx_t.shape[1]
    tm = 512
    while n % tm:
        tm //= 2
    tri = jnp.asarray(np.triu(np.ones((tm, tm), np.float32), 1), BF16)
    return pl.pallas_call(
        _rank_kernel,
        grid=(n // tm,),
        in_specs=[pl.BlockSpec((SUBLANES, tm), lambda i: (0, i)), pl.BlockSpec((tm, tm), lambda i: (0, 0))],
        out_specs=[pl.BlockSpec((SUBLANES, tm), lambda i: (0, i)), pl.BlockSpec((N_EXPERTS, LANES), lambda i: (0, 0))],
        out_shape=[jax.ShapeDtypeStruct((SUBLANES, n), jnp.int32), jax.ShapeDtypeStruct((N_EXPERTS, LANES), jnp.int32)],
        scratch_shapes=[pltpu.VMEM((N_EXPERTS, LANES), F32)],
        compiler_params=_params("arbitrary"),
        name="rank",
    )(idx_t, tri)


def _route(idx_t, block):
    n_tok = idx_t.shape[1]
    n_assign = n_tok * TOP_K
    tok_bits = max(n_tok - 1, 1).bit_length()
    assert N_EXPERTS << tok_bits < 2 ** 31
    rank_t, cnt = _rank(idx_t)
    counts = cnt[:, 0]
    e_t = idx_t[:TOP_K]
    keys = jnp.sort(((e_t << tok_bits) + jnp.arange(n_tok, dtype=jnp.int32)[None, :]).reshape(-1))
    padded = (counts + block - 1) // block * block
    starts = jnp.cumsum(counts) - counts
    pends = jnp.cumsum(padded)
    pstarts = pends - padded
    experts = jnp.arange(N_EXPERTS, dtype=jnp.int32)[:, None, None]
    dest_t = jnp.sum(jnp.where(e_t[None] == experts, pstarts[:, None, None], 0), axis=0) + rank_t[:TOP_K]
    n_blocks = -(-(n_assign + N_EXPERTS * (block - 1)) // block)
    n_rows = n_blocks * block
    block_start = jnp.arange(n_blocks, dtype=jnp.int32) * block
    block_expert = jnp.minimum(jnp.sum((pends[None, :] <= block_start[:, None]).astype(jnp.int32), axis=1),
                               N_EXPERTS - 1)
    local = block_start - pstarts[block_expert]
    first = jnp.repeat(starts[block_expert] + local, block)
    live = jnp.repeat(counts[block_expert] - local, block)
    within = jnp.tile(jnp.arange(block, dtype=jnp.int32), n_blocks)
    src = keys[jnp.clip(first + within, 0, n_assign - 1)] & ((1 << tok_bits) - 1)
    row = jnp.arange(n_rows, dtype=jnp.int32)
    row_tok = jnp.where(within < live, src, row % n_tok)
    n_used = (pends[-1] // block).astype(jnp.int32).reshape(1)
    return row_tok, dest_t, block_expert, n_used


def _mixer(x, p, caches, conv0, h0):
    batch, t, d = x.shape
    prompt = caches is None
    x2d = x.reshape(batch * t, d)
    os_, ls_, new_kv = [], [], []
    if prompt:
        nat = N_NATURAL_BLOCKS * GROUP_W
        z, zd = _in_proj_prompt(x2d, p["norm_mix"], p["w_in"], t, nat, ATT_DILATIONS[1:])
        zq = [z.reshape(batch, 1, t, nat)] + list(zd)
        for g in range(N_GROUPS):
            dil = ATT_DILATIONS[g]
            o, l = _attn_prompt(zq[g], g, COL_QKV if g == 0 else 0)
            keep = min(ATT_WINDOWS[g], t)
            if g == 0:
                kcol = (COL_QKV + 1) * GROUP_W
                last = z.reshape(batch, t, nat)[:, t - keep:, kcol:kcol + 2 * GROUP_W]
            else:
                last = zq[g][:, :, (t - keep) // dil:, GROUP_W:3 * GROUP_W]
                last = jnp.swapaxes(last, 1, 2).reshape(batch, keep, 2 * GROUP_W)
            kv = last.astype(F32).reshape(batch, keep, 2, HEADS_PER_GROUP, HEAD_DIM)
            os_.append(o)
            ls_.append(l)
            new_kv.append(kv[None])
        conv0 = jnp.zeros((batch, CONV_WIDTH - 1, d), F32)
        h0 = jnp.zeros((batch, d), F32)
    else:
        z = _in_proj(x2d, p["norm_mix"], p["w_in"], F32, 4)
        for g in range(N_GROUPS):
            o, l, kv = _attn_sample(z, caches[g], g)
            os_.append(o)
            ls_.append(l)
            new_kv.append(kv[None])
    yb, new_conv, h_last = _rglru(z.reshape(batch, t, -1), conv0, h0, p, prompt)
    x1, xn, idx, gate = _mix(os_, ls_, yb.reshape(batch * t, d), z, x2d, p)
    return x1, xn, idx, gate, new_kv, new_conv[None], h_last[None]


def kernel(x_prompt, x_sample, cache_kv_g1, cache_kv_g2, cache_kv_g3, state_conv, state_h, norm_mix, w_in, w_pa,
           w_pb, w_o, conv_w, conv_b, lru_wa, lru_ba, lru_wi, lru_bi, lru_lambda, norm_ffn, w_router, b_router,
           w1, b1, w2, b2, norm_final):
    d = x_prompt.shape[-1]
    gsz = N_GROUPS * GROUP_W

    def block_diag(w):
        per = MXU_DIM // LRU_BLOCK_W
        w = w.reshape(-1, per, LRU_BLOCK_W, LRU_BLOCK_W)
        eye = jnp.eye(per, dtype=w.dtype)
        return jnp.einsum("gacd,ab->gacbd", w, eye).reshape(-1, MXU_DIM, MXU_DIM).astype(BF16)

    w_in0 = w_in[0]
    qkv = [w_in0[:, which * gsz + g * GROUP_W: which * gsz + (g + 1) * GROUP_W]
           for g in range(N_GROUPS) for which in range(3)]
    n_e, dff2 = b1.shape[1], b1.shape[2]
    half = MXU_DIM // 2
    p = dict(
        norm_mix=norm_mix[0][None], norm_ffn=norm_ffn[0][None],
        w_in=jnp.concatenate([w_in0[:, 3 * gsz:]] + qkv, axis=1).astype(BF16),
        w_pa=w_pa[0].astype(BF16), w_pb=w_pb[0].astype(BF16), w_o=w_o[0].astype(BF16),
        conv_w=conv_w[0], conv_b=conv_b[0][None],
        wa=block_diag(lru_wa[0]), wi=block_diag(lru_wi[0]),
        ba=lru_ba[0].reshape(1, d), bi=lru_bi[0].reshape(1, d), lam=lru_lambda[0][None],
        w_router=jnp.pad(w_router[0], ((0, 0), (0, LANES - N_EXPERTS))).astype(BF16),
        b_router=jnp.pad(b_router[0], (0, LANES - N_EXPERTS))[None],
        w1=w1[0], w2=w2[0],
        b1=b1[0].reshape(n_e, dff2 // MXU_DIM, half, 2).transpose(0, 1, 3, 2).reshape(n_e, 1, dff2),
        b2=b2[0][:, None, :],
    )

    nf = norm_final[None]

    def moe(x1, xn, idx, gate, block, shape):
        row_tok, dest, block_expert, n_used = _route(idx, block)
        y_rows = _experts(xn[row_tok], block_expert, n_used, p, block)
        return _combine([y_rows[dest[kk]] for kk in range(TOP_K)], gate, x1, nf).reshape(shape)

    x1_p, xn_p, idx_p, gate_p, kv_p, conv_p, h_p = _mixer(x_prompt, p, None, None, None)
    y_p = moe(x1_p, xn_p, idx_p, gate_p, MOE_BLOCK, x_prompt.shape)
    x1_s, xn_s, idx_s, gate_s, kv_s, conv_s, h_s = _mixer(
        x_sample, p, [cache_kv_g1[0], cache_kv_g2[0], cache_kv_g3[0]], state_conv[0], state_h[0])
    y_s = moe(x1_s, xn_s, idx_s, gate_s, MOE_BLOCK_SMALL, x_sample.shape)
    return (y_p, y_s, kv_p[0], kv_p[1], kv_p[2], conv_p, h_p, kv_s[0], kv_s[1], kv_s[2], conv_s, h_s)
```

```python
import functools

import numpy as np
import jax
import jax.numpy as jnp
from jax import lax
from jax.experimental import pallas as pl
from jax.experimental.pallas import tpu as pltpu

F32 = jnp.float32
BF16 = jnp.bfloat16

N_GROUPS = 3
HEADS_PER_GROUP = 8
HEAD_DIM = 64
GROUP_W = HEADS_PER_GROUP * HEAD_DIM
ATT_WINDOWS = (128, 512, 2048)
ATT_DILATIONS = (1, 4, 16)
ATT_BLOCK = 128
N_ATT_HEADS = N_GROUPS * HEADS_PER_GROUP
CONV_WIDTH = 4
LRU_C = 8.0
LRU_BLOCK_W = 64
N_EXPERTS = 32
TOP_K = 4
SWIGLU_ALPHA = 1.702
SWIGLU_LIMIT = 7.0
RMS_EPS = 1e-6
NEG_INF = float("-inf")

LANES = 128
SUBLANES = 8
MXU_DIM = 256
VMEM_LIMIT_BYTES = 56 * 1024 * 1024

COL_QKV = 8
N_NATURAL_BLOCKS = COL_QKV + 3
LANE_GROUPS = GROUP_W // LANES
MOE_BLOCK = 512
PROJ_TILE = 512
ATT_UNITS = 2

_SLOPES = [float(np.float32(2.0 ** (-8.0 * (i + 1) / N_ATT_HEADS))) for i in range(N_ATT_HEADS)]


def _params(*sem):
    return pltpu.CompilerParams(dimension_semantics=sem, vmem_limit_bytes=VMEM_LIMIT_BYTES)


def _sigmoid(x):
    return 0.5 * jnp.tanh(0.5 * x) + 0.5


def _log2(n):
    assert n > 0 and n & (n - 1) == 0, n
    return n.bit_length() - 1


def _rmsnorm_bf16(x, gain):
    ms = jnp.mean(x * x, axis=-1, keepdims=True)
    return (x * lax.rsqrt(ms + RMS_EPS) * gain).astype(BF16)


def _in_proj_kernel(x_ref, g_ref, w_ref, o_ref, xn_ref):
    @pl.when(pl.program_id(1) == 0)
    def _():
        xn_ref[...] = _rmsnorm_bf16(x_ref[...], g_ref[...])

    o_ref[...] = jnp.dot(xn_ref[...], w_ref[...], preferred_element_type=F32).astype(o_ref.dtype)


def _in_proj(x2d, gain, w, out_dtype, n_col_tiles):
    n, d = x2d.shape
    cols = w.shape[1]
    tm = min(PROJ_TILE, n)
    tn = cols // n_col_tiles
    return pl.pallas_call(
        _in_proj_kernel,
        grid=(n // tm, n_col_tiles),
        in_specs=[
            pl.BlockSpec((tm, d), lambda i, j: (i, 0)),
            pl.BlockSpec((1, d), lambda i, j: (0, 0)),
            pl.BlockSpec((d, tn), lambda i, j: (0, j)),
        ],
        out_specs=pl.BlockSpec((tm, tn), lambda i, j: (i, j)),
        out_shape=jax.ShapeDtypeStruct((n, cols), out_dtype),
        scratch_shapes=[pltpu.VMEM((tm, d), BF16)],
        compiler_params=_params("parallel", "arbitrary"),
        name="in_proj",
    )(x2d, gain, w)


def _in_proj_prompt_kernel(x_ref, g_ref, w_ref, *refs, dilations, n_chunks):
    n_dil = len(dilations)
    perm_refs, zn_ref, zd_refs = refs[:n_dil], refs[n_dil], refs[n_dil + 1:]
    xn = _rmsnorm_bf16(x_ref[...], g_ref[...])
    nat = zn_ref.shape[1]
    cw = nat // n_chunks
    for c in range(n_chunks):
        cs = slice(c * cw, (c + 1) * cw)
        zn_ref[:, cs] = jnp.dot(xn, w_ref[:, cs], preferred_element_type=F32).astype(zn_ref.dtype)
    col = nat
    for perm_ref, zd_ref, dil in zip(perm_refs, zd_refs, dilations):
        width = zd_ref.shape[-1]
        xp = jnp.dot(perm_ref[...], xn, preferred_element_type=F32).astype(BF16)
        res = jnp.dot(xp, w_ref[:, col:col + width], preferred_element_type=F32).astype(zd_ref.dtype)
        per = res.shape[0] // dil
        for r in range(dil):
            zd_ref[r] = res[r * per:(r + 1) * per, :]
        col += width


def _in_proj_prompt(x2d, gain, w, seq, nat, dilations):
    n, d = x2d.shape
    cols = w.shape[1]
    tm = PROJ_TILE // 2
    width = 3 * GROUP_W
    assert nat + width * len(dilations) == cols and seq % tm == 0
    tiles_per_seq = seq // tm
    perms = []
    for dil in dilations:
        per = tm // dil
        o = np.arange(tm)
        perm = np.zeros((tm, tm), np.float32)
        perm[o, (o % per) * dil + o // per] = 1.0
        perms.append(jnp.asarray(perm, BF16))
    const = lambda shape: pl.BlockSpec(shape, lambda i: (0,) * len(shape))
    outs = pl.pallas_call(
        functools.partial(_in_proj_prompt_kernel, dilations=tuple(dilations), n_chunks=2),
        grid=(n // tm,),
        in_specs=[pl.BlockSpec((tm, d), lambda i: (i, 0)), const((1, d)),
                  pl.BlockSpec((d, cols), lambda i: (0, 0), pipeline_mode=pl.Buffered(1))]
        + [const((tm, tm))] * len(dilations),
        out_specs=[pl.BlockSpec((tm, nat), lambda i: (i, 0))]
        + [pl.BlockSpec((None, dil, tm // dil, width), lambda i: (i // tiles_per_seq, 0, i % tiles_per_seq, 0))
           for dil in dilations],
        out_shape=[jax.ShapeDtypeStruct((n, nat), BF16)]
        + [jax.ShapeDtypeStruct((n // seq, dil, seq // dil, width), BF16) for dil in dilations],
        compiler_params=_params("parallel"),
        name="in_proj_prompt",
    )(x2d, gain, w, *perms)
    return outs[0], outs[1:]


def _attn_prompt_kernel(q_ref, kp_ref, kc_ref, vp_ref, vc_ref, b_ref, o_ref, l_ref, *, dilation, n_sub, units):
    blk = ATT_BLOCK
    j = pl.program_id(1)
    low = lax.broadcasted_iota(jnp.int32, (blk, LANES), 1) < HEAD_DIM

    def unit(r, u):
        own = slice(u * blk, (u + 1) * blk)
        q = q_ref[r, own, :] * (HEAD_DIM ** -0.5)
        if u == 0:
            k = jnp.concatenate([kp_ref[r], kc_ref[r, own, :]], axis=0)
            v = jnp.concatenate([vp_ref[r], vc_ref[r, own, :]], axis=0)
            bias = b_ref.at[jnp.minimum(j, 1)]
        else:
            both = slice((u - 1) * blk, (u + 1) * blk)
            k, v = kc_ref[r, both, :], vc_ref[r, both, :]
            bias = b_ref.at[1]
        rows = pl.ds(u * blk * dilation + r, blk, stride=dilation) if dilation > 1 else own
        for p in range(LANE_GROUPS):
            cs = slice(p * LANES, (p + 1) * LANES)
            qg, kg, vg = q[:, cs], k[:, cs], v[:, cs]
            outs, lses = [], []
            for hh in range(2):
                sel = low if hh == 0 else jnp.logical_not(low)
                qm = jnp.where(sel, qg, jnp.zeros_like(qg))
                s = lax.dot_general(qm, kg, (((1,), (1,)), ((), ())), preferred_element_type=F32)
                s = s + bias[2 * p + hh]
                m = jnp.max(s, axis=-1, keepdims=True)
                e = jnp.exp(s - m)
                den = jnp.sum(e, axis=-1, keepdims=True)
                outs.append(jnp.dot(e.astype(BF16), vg, preferred_element_type=F32) / den)
                lses.append(m + jnp.log(den))
            o_ref[p, rows, :] = jnp.where(low, outs[0], outs[1])
            l_ref[p, rows, :] = jnp.where(low, lses[0], lses[1])

    per_r = min(units, n_sub)
    r_per_body = units // per_r
    assert n_sub % per_r == 0 and dilation % r_per_body == 0

    def body(i, carry):
        for rr in range(r_per_body):
            for u in range(n_sub):
                unit(i * r_per_body + rr, u)
        return carry

    if dilation == r_per_body:
        body(0, 0)
    else:
        lax.fori_loop(0, dilation // r_per_body, body, 0)


def _band_bias(g):
    blk, d = ATT_BLOCK, ATT_DILATIONS[g]
    qq = np.arange(blk)[:, None]
    kk = np.arange(2 * blk)[None, :]
    step = qq + blk - kk
    band = (step >= 0) & (step <= blk)
    slopes = np.asarray(_SLOPES[g * HEADS_PER_GROUP:(g + 1) * HEADS_PER_GROUP], np.float32)
    bias = -slopes[:, None, None] * (step * d).astype(np.float32)[None]
    out = np.empty((2, HEADS_PER_GROUP, blk, 2 * blk), np.float32)
    out[0] = np.where(band & (kk >= blk), bias, -np.inf)
    out[1] = np.where(band, bias, -np.inf)
    return jnp.asarray(out)


def _attn_prompt(zq, g, col0):
    d = ATT_DILATIONS[g]
    batch, _, n, _ = zq.shape
    n_sub = max(1, ATT_UNITS // d)
    nb = n // (ATT_BLOCK * n_sub)
    span = ATT_BLOCK * n_sub

    def cur(c):
        return pl.BlockSpec((None, d, span, GROUP_W), lambda b, j: (b, 0, j, c))

    def prev(c):
        return pl.BlockSpec((None, d, ATT_BLOCK, GROUP_W), lambda b, j: (b, 0, jnp.maximum(j * n_sub - 1, 0), c))

    out_spec = pl.BlockSpec((LANE_GROUPS, span * d, LANES), lambda b, j: (0, b * nb + j, 0))
    out_sds = jax.ShapeDtypeStruct((LANE_GROUPS, batch * n * d, LANES), F32)
    bias = _band_bias(g)
    return pl.pallas_call(
        functools.partial(_attn_prompt_kernel, dilation=d, n_sub=n_sub, units=ATT_UNITS),
        grid=(batch, nb),
        in_specs=[cur(col0), prev(col0 + 1), cur(col0 + 1), prev(col0 + 2), cur(col0 + 2),
                  pl.BlockSpec(bias.shape, lambda b, j: (0, 0, 0, 0))],
        out_specs=[out_spec, out_spec],
        out_shape=[out_sds, out_sds],
        compiler_params=_params("parallel", "arbitrary"),
        name=f"attn_prompt_g{g}",
    )(zq, zq, zq, zq, zq, bias)


def _attn_sample_kernel(q_ref, k_ref, v_ref, c_ref, sl_ref, o_ref, l_ref, cout_ref, *, window, dilation):
    t_new = q_ref.shape[0]
    rows = HEADS_PER_GROUP * t_new
    q = q_ref[...] * (HEAD_DIM ** -0.5)
    pad = jnp.zeros((LANES - t_new, GROUP_W), F32)
    kn = jnp.concatenate([k_ref[...], pad], axis=0)
    vn = jnp.concatenate([v_ref[...], pad], axis=0)
    new_t = jnp.concatenate([kn.T, vn.T], axis=0)
    cin = c_ref[...]
    cout_ref[...] = pltpu.roll(cin, window - t_new, axis=1)
    tail = cout_ref[:, window - LANES:window]
    lane = lax.broadcasted_iota(jnp.int32, tail.shape, 1)
    cout_ref[:, window - LANES:window] = jnp.where(lane >= LANES - t_new,
                                                   pltpu.roll(new_t, LANES - t_new, axis=1), tail)
    qt = jnp.concatenate([q] * HEADS_PER_GROUP, axis=0)
    row_h = lax.broadcasted_iota(jnp.int32, (rows, GROUP_W), 0) >> _log2(t_new)
    col_h = lax.broadcasted_iota(jnp.int32, (rows, GROUP_W), 1) >> _log2(HEAD_DIM)
    qbd = jnp.where(row_h == col_h, qt, 0.0).astype(BF16)
    k_t = cin[0:GROUP_W, :].astype(BF16)
    v_t = cin[GROUP_W:2 * GROUP_W, :].astype(BF16)
    s_c = jnp.dot(qbd, k_t, preferred_element_type=F32)
    s_n = jnp.dot(qbd, new_t[0:GROUP_W, :].astype(BF16), preferred_element_type=F32)
    _log2(dilation)
    slope = sl_ref[:, 0:1]
    t_c = lax.broadcasted_iota(jnp.int32, s_c.shape, 0) & (t_new - 1)
    dist_c = window + t_c - lax.broadcasted_iota(jnp.int32, s_c.shape, 1)
    s_c = jnp.where((dist_c <= window) & ((dist_c & (dilation - 1)) == 0), s_c - slope * dist_c.astype(F32), NEG_INF)
    t_n = lax.broadcasted_iota(jnp.int32, s_n.shape, 0) & (t_new - 1)
    dist_n = t_n - lax.broadcasted_iota(jnp.int32, s_n.shape, 1)
    s_n = jnp.where((dist_n >= 0) & ((dist_n & (dilation - 1)) == 0), s_n - slope * dist_n.astype(F32), NEG_INF)
    m = jnp.maximum(jnp.max(s_c, axis=-1, keepdims=True), jnp.max(s_n, axis=-1, keepdims=True))
    e_c = jnp.exp(s_c - m)
    e_n = jnp.exp(s_n - m)
    den = jnp.sum(e_c, axis=-1, keepdims=True) + jnp.sum(e_n, axis=-1, keepdims=True)
    o_full = lax.dot_general(e_c.astype(BF16), v_t, (((1,), (1,)), ((), ())), preferred_element_type=F32)
    o_full = (o_full + jnp.dot(e_n.astype(BF16), vn.astype(BF16), preferred_element_type=F32)) / den
    lse = m + jnp.log(den)
    out_h = lax.broadcasted_iota(jnp.int32, (t_new, GROUP_W), 1) >> _log2(HEAD_DIM)
    o = jnp.zeros((t_new, GROUP_W), F32)
    l = jnp.zeros((t_new, GROUP_W), F32)
    for h in range(HEADS_PER_GROUP):
        rs = slice(h * t_new, (h + 1) * t_new)
        o = jnp.where(out_h == h, o_full[rs, :], o)
        l = jnp.where(out_h == h, lse[rs, :], l)
    for c in range(LANE_GROUPS):
        o_ref[c] = o[:, c * LANES:(c + 1) * LANES]
        l_ref[c] = l[:, c * LANES:(c + 1) * LANES]


def _attn_sample(z, cache, g):
    batch, window = cache.shape[0], cache.shape[1]
    n_tok = z.shape[0]
    t_new = n_tok // batch
    d = ATT_DILATIONS[g]
    c_t = jnp.transpose(cache, (0, 2, 3, 4, 1)).reshape(batch, 2 * GROUP_W, window)
    slopes = np.repeat(np.asarray(_SLOPES[g * HEADS_PER_GROUP:(g + 1) * HEADS_PER_GROUP], np.float32), t_new)
    slopes = jnp.asarray(np.broadcast_to(slopes[:, None], (HEADS_PER_GROUP * t_new, LANES)))
    col0 = COL_QKV + 3 * g

    def col(c):
        return pl.BlockSpec((t_new, GROUP_W), lambda b: (b, c))

    tok_spec = pl.BlockSpec((LANE_GROUPS, t_new, LANES), lambda b: (0, b, 0))
    tok_sds = jax.ShapeDtypeStruct((LANE_GROUPS, n_tok, LANES), F32)
    buf_spec = pl.BlockSpec((None, 2 * GROUP_W, window), lambda b: (b, 0, 0))
    o, l, cout = pl.pallas_call(
        functools.partial(_attn_sample_kernel, window=window, dilation=d),
        grid=(batch,),
        in_specs=[col(col0), col(col0 + 1), col(col0 + 2), buf_spec, pl.BlockSpec(slopes.shape, lambda b: (0, 0))],
        out_specs=[tok_spec, tok_spec, buf_spec],
        out_shape=[tok_sds, tok_sds, jax.ShapeDtypeStruct(c_t.shape, F32)],
        compiler_params=_params("parallel"),
        name=f"attn_sample_g{g}",
    )(z, z, z, c_t, slopes)
    cout = cout.reshape(batch, 2, HEADS_PER_GROUP, HEAD_DIM, window)
    return o, l, jnp.transpose(cout, (0, 4, 1, 2, 3))


def _gelu_tanh(x):
    c1 = float(np.sqrt(2.0 / np.pi))
    return x * (0.5 * jnp.tanh(x * (c1 + (c1 * 0.044715) * (x * x))) + 0.5)


def _rglru_kernel(xb_ref, yg_ref, c0_ref, h0_ref, cw_ref, cb_ref, wa_ref, wi_ref, ba_ref, bi_ref, lam_ref,
                  yb_ref, cout_ref, hout_ref, xpad, a_s, u_s, h_s, *, tc, seg, reset_first):
    nb = xb_ref.shape[0]
    ngl = a_s.shape[0]
    j = pl.program_id(1)

    @pl.when(j == 0)
    def _():
        xpad[:, 0:SUBLANES, :] = c0_ref[...]
        h_s[...] = h0_ref[...]

    @pl.when(j > 0)
    def _():
        xpad[:, 0:SUBLANES, :] = xpad[:, tc:tc + SUBLANES, :]

    xpad[:, SUBLANES:SUBLANES + tc, :] = xb_ref[...].astype(F32)
    cout_ref[...] = xpad[:, tc:tc + SUBLANES, :]

    lam = lam_ref[...]
    softplus_neg = jnp.maximum(-lam, 0.0) + jnp.log1p(jnp.exp(-jnp.abs(lam)))
    first = (lax.broadcasted_iota(jnp.int32, (tc, 1), 0) == 0) & (j == 0)
    gw = wa_ref.shape[1]
    for b in range(nb):
        xc = cb_ref[...]
        for tap in range(CONV_WIDTH):
            off = SUBLANES - (CONV_WIDTH - 1) + tap
            xc = xc + xpad[b, off:off + tc, :] * cw_ref[tap:tap + 1, :]
        xcb = xc.astype(BF16)
        ra, ri = [], []
        for blk in range(wa_ref.shape[0]):
            xs = xcb[:, blk * gw:(blk + 1) * gw]
            ra.append(jnp.dot(xs, wa_ref[blk], preferred_element_type=F32))
            ri.append(jnp.dot(xs, wi_ref[blk], preferred_element_type=F32))
        r = _sigmoid(jnp.concatenate(ra, axis=-1) + ba_ref[...])
        i = _sigmoid(jnp.concatenate(ri, axis=-1) + bi_ref[...])
        log_a = -LRU_C * r * softplus_neg
        a = jnp.exp(log_a)
        w = 1.0 - a * a
        mult = jnp.where(w > 0.0, w * lax.rsqrt(w), 0.0)
        if reset_first:
            mult = jnp.where(first, 1.0, mult)
        u = mult * (i * xc)
        for c in range(ngl):
            a_s[c, b * seg:b * seg + tc, :] = a[:, c * LANES:(c + 1) * LANES]
            u_s[c, b * seg:b * seg + tc, :] = u[:, c * LANES:(c + 1) * LANES]

    def step(t, hs):
        idx = pl.ds(t, nb, stride=seg)
        out = []
        for c in range(ngl):
            h = a_s[c, idx, :] * hs[c] + u_s[c, idx, :]
            u_s[c, idx, :] = h
            out.append(h)
        return tuple(out)

    hs = lax.fori_loop(0, tc, step, tuple(h_s[:, c * LANES:(c + 1) * LANES] for c in range(ngl)))
    h = jnp.concatenate(hs, axis=-1)
    h_s[...] = h
    hout_ref[...] = h
    for b in range(nb):
        hb = jnp.concatenate([u_s[c, b * seg:b * seg + tc, :] for c in range(ngl)], axis=-1)
        yb_ref[b] = (_gelu_tanh(yg_ref[b].astype(F32)) * hb).astype(yb_ref.dtype)


def _rglru(z3, conv0, h0, p, reset_first):
    batch, t, _ = z3.shape
    width = h0.shape[-1]
    nb = SUBLANES
    tc = min(128, t)
    seg = tc + SUBLANES
    c0 = jnp.pad(conv0, ((0, 0), (SUBLANES - (CONV_WIDTH - 1), 0), (0, 0)))
    cw = jnp.pad(p["conv_w"], ((0, SUBLANES - CONV_WIDTH), (0, 0)))
    ngrp = p["wa"].shape[0]
    gw = p["wa"].shape[1]

    def full(shape):
        return pl.BlockSpec(shape, lambda i, j: (0,) * len(shape))

    yb, cout, hout = pl.pallas_call(
        functools.partial(_rglru_kernel, tc=tc, seg=seg, reset_first=reset_first),
        grid=(batch // nb, t // tc),
        in_specs=[
            pl.BlockSpec((nb, tc, width), lambda i, j: (i, j, 0)),
            pl.BlockSpec((nb, tc, width), lambda i, j: (i, j, 1)),
            pl.BlockSpec((nb, SUBLANES, width), lambda i, j: (i, 0, 0)),
            pl.BlockSpec((nb, width), lambda i, j: (i, 0)),
            full((SUBLANES, width)), full((1, width)),
            full((ngrp, gw, gw)), full((ngrp, gw, gw)),
            full((1, width)), full((1, width)), full((1, width)),
        ],
        out_specs=[
            pl.BlockSpec((nb, tc, width), lambda i, j: (i, j, 0)),
            pl.BlockSpec((nb, SUBLANES, width), lambda i, j: (i, 0, 0)),
            pl.BlockSpec((nb, width), lambda i, j: (i, 0)),
        ],
        out_shape=[
            jax.ShapeDtypeStruct((batch, t, width), BF16),
            jax.ShapeDtypeStruct((batch, SUBLANES, width), F32),
            jax.ShapeDtypeStruct((batch, width), F32),
        ],
        scratch_shapes=[
            pltpu.VMEM((nb, tc + SUBLANES, width), F32),
            pltpu.VMEM((width // LANES, nb * seg, LANES), F32),
            pltpu.VMEM((width // LANES, nb * seg, LANES), F32),
            pltpu.VMEM((nb, width), F32),
        ],
        compiler_params=_params("parallel", "arbitrary"),
        name="rglru",
    )(z3, z3, c0, h0, cw, p["conv_b"], p["wa"], p["wi"], p["ba"], p["bi"], p["lam"])
    return yb, cout[:, SUBLANES - (CONV_WIDTH - 1):, :], hout


def _mix_kernel(o1, o2, o3, l1, l2, l3, yb_ref, ga_ref, gb_ref, x_ref, wpa_ref, wpb_ref, wo_ref, nf_ref,
                wr_ref, br_ref, x1_ref, xn_ref, idx_ref, gate_ref):
    parts = []
    for c in range(LANE_GROUPS):
        la, lb, lc = l1[c], l2[c], l3[c]
        m = jnp.maximum(jnp.maximum(la, lb), lc)
        ea, eb, ec = jnp.exp(la - m), jnp.exp(lb - m), jnp.exp(lc - m)
        inv = 1.0 / (ea + eb + ec)
        parts.append((ea * inv) * o1[c] + (eb * inv) * o2[c] + (ec * inv) * o3[c])
    att = jnp.concatenate(parts, axis=-1)
    y_a = jnp.dot(att.astype(BF16), wpa_ref[...], preferred_element_type=F32)
    y_b = jnp.dot(yb_ref[...], wpb_ref[...], preferred_element_type=F32)
    merged = _sigmoid(ga_ref[...].astype(F32)) * y_a + _sigmoid(gb_ref[...].astype(F32)) * y_b
    x1 = x_ref[...] + jnp.dot(merged.astype(BF16), wo_ref[...], preferred_element_type=F32)
    x1_ref[...] = x1
    ms = jnp.mean(x1 * x1, axis=-1, keepdims=True)
    xn = (x1 * lax.rsqrt(ms + RMS_EPS) * nf_ref[...]).astype(BF16)
    xn_ref[...] = xn
    logits = jnp.dot(xn, wr_ref[...], preferred_element_type=F32) + br_ref[...]
    lane = lax.broadcasted_iota(jnp.int32, logits.shape, 1)
    lane_f = lane.astype(F32)
    logits = jnp.where(lane < N_EXPERTS, logits, NEG_INF)
    vals, idxs = [], []
    for _ in range(TOP_K):
        mk = jnp.max(logits, axis=-1, keepdims=True)
        ik = jnp.min(jnp.where(logits == mk, lane_f, float(LANES)), axis=-1, keepdims=True)
        logits = jnp.where(lane_f == ik, NEG_INF, logits)
        vals.append(mk)
        idxs.append(ik)
    es = [jnp.exp(vk - vals[0]) for vk in vals]
    tot = es[0] + es[1] + es[2] + es[3]
    idx_out = jnp.zeros(logits.shape, F32)
    gate_out = jnp.zeros(logits.shape, F32)
    for kk in range(TOP_K):
        idx_out = jnp.where(lane == kk, idxs[kk], idx_out)
        gate_out = jnp.where(lane == kk, es[kk] / tot, gate_out)
    idx_ref[...] = jnp.transpose(idx_out)[0:SUBLANES, :].astype(jnp.int32)
    gate_ref[...] = gate_out


def _mix(os_, ls_, yb, z, x2d, p):
    n, d = x2d.shape
    tm = min(512, n)

    def rows(w, c=0):
        return pl.BlockSpec((tm, w), lambda i: (i, c))

    def full(a):
        return pl.BlockSpec(a.shape, lambda i: (0,) * a.ndim)

    grp = pl.BlockSpec((LANE_GROUPS, tm, LANES), lambda i: (0, i, 0))
    in_specs = [grp] * 6 + [rows(d), rows(d, 2), rows(d, 3), rows(d),
                            full(p["w_pa"]), full(p["w_pb"]), full(p["w_o"]), full(p["norm_ffn"]),
                            full(p["w_router"]), full(p["b_router"])]
    args = [*os_, *ls_, yb, z, z, x2d, p["w_pa"], p["w_pb"], p["w_o"], p["norm_ffn"], p["w_router"], p["b_router"]]
    x1, xn, idx, gate = pl.pallas_call(
        _mix_kernel,
        grid=(n // tm,),
        in_specs=in_specs,
        out_specs=[rows(d), rows(d),
                   pl.BlockSpec((SUBLANES, tm), lambda i: (0, i)), rows(LANES)],
        out_shape=[jax.ShapeDtypeStruct((n, d), F32), jax.ShapeDtypeStruct((n, d), BF16),
                   jax.ShapeDtypeStruct((SUBLANES, n), jnp.int32), jax.ShapeDtypeStruct((n, LANES), F32)],
        compiler_params=_params("parallel"),
        name="mix",
    )(*args)
    return x1, xn, idx, gate


def _expert_kernel(be_ref, nused_ref, x_ref, w1_ref, b1_ref, w2_ref, b2_ref, perm_ref, y_ref, w1_s, w2_s):
    i = pl.program_id(0)
    dff = w2_ref.shape[0]
    n_cb = w1_ref.shape[1] // MXU_DIM
    half = MXU_DIM // 2

    @pl.when((i == 0) | (be_ref[i] != be_ref[jnp.maximum(i - 1, 0)]))
    def _():
        for cb in range(n_cb):
            cs = slice(cb * MXU_DIM, (cb + 1) * MXU_DIM)
            w1_s[:, cs] = jnp.dot(w1_ref[:, cs].astype(BF16), perm_ref[...], preferred_element_type=F32).astype(BF16)
        w2_s[...] = w2_ref[...].astype(BF16)

    @pl.when(i < nused_ref[0])
    def _():
        h = jnp.dot(x_ref[...], w1_s[...], preferred_element_type=F32) + b1_ref[...]
        acts = []
        for cb in range(n_cb):
            x_glu = jnp.minimum(h[:, cb * MXU_DIM:cb * MXU_DIM + half], SWIGLU_LIMIT)
            x_lin = jnp.clip(h[:, cb * MXU_DIM + half:(cb + 1) * MXU_DIM], -SWIGLU_LIMIT, SWIGLU_LIMIT)
            acts.append(x_glu * _sigmoid(SWIGLU_ALPHA * x_glu) * (x_lin + 1.0))
        act = jnp.concatenate(acts, axis=-1)
        assert act.shape[-1] == dff
        y = jnp.dot(act.astype(BF16), w2_s[...], preferred_element_type=F32) + b2_ref[...]
        y_ref[...] = y.astype(y_ref.dtype)

    @pl.when(i >= nused_ref[0])
    def _():
        y_ref[...] = jnp.zeros_like(y_ref)


def _experts(x_rows, block_expert, n_used, p, block):
    n_rows, d = x_rows.shape
    n_blocks = n_rows // block
    dff2 = p["w1"].shape[-1]
    half = MXU_DIM // 2
    o = np.arange(MXU_DIM)
    perm = np.zeros((MXU_DIM, MXU_DIM), np.float32)
    perm[np.where(o < half, 2 * o, 2 * (o - half) + 1), o] = 1.0
    grid_spec = pltpu.PrefetchScalarGridSpec(
        num_scalar_prefetch=2,
        grid=(n_blocks,),
        in_specs=[
            pl.BlockSpec((block, d), lambda i, be, nu: (i, 0)),
            pl.BlockSpec((None, d, dff2), lambda i, be, nu: (be[i], 0, 0)),
            pl.BlockSpec((None, 1, dff2), lambda i, be, nu: (be[i], 0, 0)),
            pl.BlockSpec((None, dff2 // 2, d), lambda i, be, nu: (be[i], 0, 0)),
            pl.BlockSpec((None, 1, d), lambda i, be, nu: (be[i], 0, 0)),
            pl.BlockSpec((MXU_DIM, MXU_DIM), lambda i, be, nu: (0, 0)),
        ],
        out_specs=pl.BlockSpec((block, d), lambda i, be, nu: (i, 0)),
        scratch_shapes=[pltpu.VMEM((d, dff2), BF16), pltpu.VMEM((dff2 // 2, d), BF16)],
    )
    return pl.pallas_call(
        _expert_kernel,
        grid_spec=grid_spec,
        out_shape=jax.ShapeDtypeStruct((n_rows, d), BF16),
        compiler_params=_params("arbitrary"),
        name="experts",
    )(block_expert, n_used, x_rows, p["w1"], p["b1"], p["w2"], p["b2"], jnp.asarray(perm, BF16))


def _combine_kernel(y0, y1, y2, y3, gate_ref, x1_ref, nf_ref, *rest):
    o_ref = rest[-1]
    g = gate_ref[...]
    y = jnp.zeros(x1_ref.shape, F32)
    for kk, y_ref in enumerate((y0, y1, y2, y3)):
        y = y + y_ref[...].astype(F32) * g[:, kk:kk + 1]
    x = x1_ref[...] + y
    ms = jnp.mean(x * x, axis=-1, keepdims=True)
    o_ref[...] = x * lax.rsqrt(ms + RMS_EPS) * nf_ref[...]


def _combine(ys, gate, x1, norm_final, row0=0, out_buf=None):
    n, d = ys[0].shape
    tm = min(256, n)
    assert row0 % tm == 0
    off = row0 // tm
    part = pl.BlockSpec((tm, d), lambda i: (i, 0))
    row = pl.BlockSpec((tm, d), lambda i: (i + off, 0))
    in_specs = [part] * TOP_K + [pl.BlockSpec((tm, LANES), lambda i: (i + off, 0)), row,
                                 pl.BlockSpec((1, d), lambda i: (0, 0))]
    args = [*ys, gate, x1, norm_final]
    aliases = {}
    if out_buf is not None:
        aliases = {len(args): 0}
        in_specs.append(pl.BlockSpec(memory_space=pl.ANY))
        args.append(out_buf)
    return pl.pallas_call(
        _combine_kernel,
        grid=(n // tm,),
        in_specs=in_specs,
        out_specs=row,
        out_shape=jax.ShapeDtypeStruct(x1.shape, F32),
        input_output_aliases=aliases,
        compiler_params=_params("parallel"),
        name="combine",
    )(*args)


def _rank_kernel(idx_ref, tri_ref, rank_ref, cnt_ref, carry):
    @pl.when(pl.program_id(0) == 0)
    def _():
        carry[...] = jnp.zeros_like(carry)

    idx = idx_ref[...]
    tm = idx.shape[1]
    expert = lax.broadcasted_iota(jnp.int32, (N_EXPERTS, tm), 0)
    hits = [expert == idx[kk:kk + 1, :] for kk in range(TOP_K)]
    onehot = jnp.zeros((N_EXPERTS, tm), F32)
    for hit in hits:
        onehot = onehot + jnp.where(hit, 1.0, 0.0)
    before = jnp.dot(onehot.astype(BF16), tri_ref[...], preferred_element_type=F32) + carry[:, 0:1]
    row = lax.broadcasted_iota(jnp.int32, idx.shape, 0)
    out = jnp.zeros(idx.shape, F32)
    for kk, hit in enumerate(hits):
        out = jnp.where(row == kk, jnp.sum(jnp.where(hit, before, 0.0), axis=0, keepdims=True), out)
    rank_ref[...] = out.astype(jnp.int32)
    carry[...] = carry[...] + jnp.sum(onehot, axis=1, keepdims=True)
    cnt_ref[...] = carry[...].astype(jnp.int32)


def _rank(idx_t):
    n = idx_t.shape[1]
    tm = 512
    while n % tm:
        tm //= 2
    tri = jnp.asarray(np.triu(np.ones((tm, tm), np.float32), 1), BF16)
    return pl.pallas_call(
        _rank_kernel,
        grid=(n // tm,),
        in_specs=[pl.BlockSpec((SUBLANES, tm), lambda i: (0, i)), pl.BlockSpec((tm, tm), lambda i: (0, 0))],
        out_specs=[pl.BlockSpec((SUBLANES, tm), lambda i: (0, i)), pl.BlockSpec((N_EXPERTS, LANES), lambda i: (0, 0))],
        out_shape=[jax.ShapeDtypeStruct((SUBLANES, n), jnp.int32), jax.ShapeDtypeStruct((N_EXPERTS, LANES), jnp.int32)],
        scratch_shapes=[pltpu.VMEM((N_EXPERTS, LANES), F32)],
        compiler_params=_params("arbitrary"),
        name="rank",
    )(idx_t, tri)


def _route(idx_t, block):
    n_tok = idx_t.shape[1]
    n_assign = n_tok * TOP_K
    tok_bits = max(n_tok - 1, 1).bit_length()
    assert N_EXPERTS << tok_bits < 2 ** 31
    rank_t, cnt = _rank(idx_t)
    counts = cnt[:, 0]
    e_t = idx_t[:TOP_K]
    keys = jnp.sort(((e_t << tok_bits) + jnp.arange(n_tok, dtype=jnp.int32)[None, :]).reshape(-1))
    padded = (counts + block - 1) // block * block
    starts = jnp.cumsum(counts) - counts
    pends = jnp.cumsum(padded)
    pstarts = pends - padded
    experts = jnp.arange(N_EXPERTS, dtype=jnp.int32)[:, None, None]
    dest_t = jnp.sum(jnp.where(e_t[None] == experts, pstarts[:, None, None], 0), axis=0) + rank_t[:TOP_K]
    n_blocks = -(-(n_assign + N_EXPERTS * (block - 1)) // block)
    n_rows = n_blocks * block
    block_start = jnp.arange(n_blocks, dtype=jnp.int32) * block
    block_expert = jnp.minimum(jnp.sum((pends[None, :] <= block_start[:, None]).astype(jnp.int32), axis=1),
                               N_EXPERTS - 1)
    local = block_start - pstarts[block_expert]
    first = jnp.repeat(starts[block_expert] + local, block)
    live = jnp.repeat(counts[block_expert] - local, block)
    within = jnp.tile(jnp.arange(block, dtype=jnp.int32), n_blocks)
    src = keys[jnp.clip(first + within, 0, n_assign - 1)] & ((1 << tok_bits) - 1)
    row = jnp.arange(n_rows, dtype=jnp.int32)
    row_tok = jnp.where(within < live, src, row % n_tok)
    n_used = (pends[-1] // block).astype(jnp.int32).reshape(1)
    return row_tok, dest_t, block_expert, n_used


def _mixer(x, p, caches, conv0, h0):
    batch, t, d = x.shape
    prompt = caches is None
    x2d = x.reshape(batch * t, d)
    os_, ls_, new_kv = [], [], []
    if prompt:
        nat = N_NATURAL_BLOCKS * GROUP_W
        z, zd = _in_proj_prompt(x2d, p["norm_mix"], p["w_in"], t, nat, ATT_DILATIONS[1:])
        zq = [z.reshape(batch, 1, t, nat)] + list(zd)
        for g in range(N_GROUPS):
            dil = ATT_DILATIONS[g]
            o, l = _attn_prompt(zq[g], g, COL_QKV if g == 0 else 0)
            keep = min(ATT_WINDOWS[g], t)
            if g == 0:
                kcol = (COL_QKV + 1) * GROUP_W
                last = z.reshape(batch, t, nat)[:, t - keep:, kcol:kcol + 2 * GROUP_W]
            else:
                last = zq[g][:, :, (t - keep) // dil:, GROUP_W:3 * GROUP_W]
                last = jnp.swapaxes(last, 1, 2).reshape(batch, keep, 2 * GROUP_W)
            kv = last.astype(F32).reshape(batch, keep, 2, HEADS_PER_GROUP, HEAD_DIM)
            os_.append(o)
            ls_.append(l)
            new_kv.append(kv[None])
        conv0 = jnp.zeros((batch, CONV_WIDTH - 1, d), F32)
        h0 = jnp.zeros((batch, d), F32)
    else:
        z = _in_proj(x2d, p["norm_mix"], p["w_in"], F32, 4)
        for g in range(N_GROUPS):
            o, l, kv = _attn_sample(z, caches[g], g)
            os_.append(o)
            ls_.append(l)
            new_kv.append(kv[None])
    yb, new_conv, h_last = _rglru(z.reshape(batch, t, -1), conv0, h0, p, prompt)
    x1, xn, idx, gate = _mix(os_, ls_, yb.reshape(batch * t, d), z, x2d, p)
    return x1, xn, idx, gate, new_kv, new_conv[None], h_last[None]


def kernel(x_prompt, x_sample, cache_kv_g1, cache_kv_g2, cache_kv_g3, state_conv, state_h, norm_mix, w_in, w_pa,
           w_pb, w_o, conv_w, conv_b, lru_wa, lru_ba, lru_wi, lru_bi, lru_lambda, norm_ffn, w_router, b_router,
           w1, b1, w2, b2, norm_final):
    d = x_prompt.shape[-1]
    gsz = N_GROUPS * GROUP_W

    def block_diag(w):
        per = MXU_DIM // LRU_BLOCK_W
        w = w.reshape(-1, per, LRU_BLOCK_W, LRU_BLOCK_W)
        eye = jnp.eye(per, dtype=w.dtype)
        return jnp.einsum("gacd,ab->gacbd", w, eye).reshape(-1, MXU_DIM, MXU_DIM).astype(BF16)

    w_in0 = w_in[0]
    qkv = [w_in0[:, which * gsz + g * GROUP_W: which * gsz + (g + 1) * GROUP_W]
           for g in range(N_GROUPS) for which in range(3)]
    n_e, dff2 = b1.shape[1], b1.shape[2]
    half = MXU_DIM // 2
    p = dict(
        norm_mix=norm_mix[0][None], norm_ffn=norm_ffn[0][None],
        w_in=jnp.concatenate([w_in0[:, 3 * gsz:]] + qkv, axis=1).astype(BF16),
        w_pa=w_pa[0].astype(BF16), w_pb=w_pb[0].astype(BF16), w_o=w_o[0].astype(BF16),
        conv_w=conv_w[0], conv_b=conv_b[0][None],
        wa=block_diag(lru_wa[0]), wi=block_diag(lru_wi[0]),
        ba=lru_ba[0].reshape(1, d), bi=lru_bi[0].reshape(1, d), lam=lru_lambda[0][None],
        w_router=jnp.pad(w_router[0], ((0, 0), (0, LANES - N_EXPERTS))).astype(BF16),
        b_router=jnp.pad(b_router[0], (0, LANES - N_EXPERTS))[None],
        w1=w1[0], w2=w2[0],
        b1=b1[0].reshape(n_e, dff2 // MXU_DIM, half, 2).transpose(0, 1, 3, 2).reshape(n_e, 1, dff2),
        b2=b2[0][:, None, :],
    )

    nf = norm_final[None]

    x1_p, xn_p, idx_p, gate_p, kv_p, conv_p, h_p = _mixer(x_prompt, p, None, None, None)
    x1_s, xn_s, idx_s, gate_s, kv_s, conv_s, h_s = _mixer(
        x_sample, p, [cache_kv_g1[0], cache_kv_g2[0], cache_kv_g3[0]], state_conv[0], state_h[0])

    n_p = x1_p.shape[0]
    half = n_p // 2
    tok_a, dest_a, be_a, nu_a = _route(idx_p[:, :half], MOE_BLOCK)
    tok_b, dest_b, be_b, nu_b = _route(jnp.concatenate([idx_p[:, half:], idx_s], axis=1), MOE_BLOCK)
    x_a = xn_p[tok_a]
    x_b = jnp.concatenate([xn_p[half:], xn_s], axis=0)[tok_b]
    y_a = _experts(x_a, be_a, nu_a, p, MOE_BLOCK)
    y_b = _experts(x_b, be_b, nu_b, p, MOE_BLOCK)
    y_p = _combine([y_a[dest_a[kk]] for kk in range(TOP_K)], gate_p, x1_p, nf)
    y_p = _combine([y_b[dest_b[kk, :n_p - half]] for kk in range(TOP_K)], gate_p, x1_p, nf, row0=half, out_buf=y_p)
    y_s = _combine([y_b[dest_b[kk, n_p - half:]] for kk in range(TOP_K)], gate_s, x1_s, nf)
    y_p, y_s = y_p.reshape(x_prompt.shape), y_s.reshape(x_sample.shape)
    return (y_p, y_s, kv_p[0], kv_p[1], kv_p[2], conv_p, h_p, kv_s[0], kv_s[1], kv_s[2], conv_s, h_s)
```

```python
import functools

import numpy as np
import jax
import jax.numpy as jnp
from jax import lax
from jax.experimental import pallas as pl
from jax.experimental.pallas import tpu as pltpu

F32 = jnp.float32
BF16 = jnp.bfloat16

N_GROUPS = 3
HEADS_PER_GROUP = 8
HEAD_DIM = 64
GROUP_W = HEADS_PER_GROUP * HEAD_DIM
ATT_WINDOWS = (128, 512, 2048)
ATT_DILATIONS = (1, 4, 16)
ATT_BLOCK = 128
N_ATT_HEADS = N_GROUPS * HEADS_PER_GROUP
CONV_WIDTH = 4
LRU_C = 8.0
LRU_BLOCK_W = 64
N_EXPERTS = 32
TOP_K = 4
SWIGLU_ALPHA = 1.702
SWIGLU_LIMIT = 7.0
RMS_EPS = 1e-6
NEG_INF = float("-inf")

LANES = 128
SUBLANES = 8
MXU_DIM = 256
VMEM_LIMIT_BYTES = 56 * 1024 * 1024

COL_QKV = 8
N_NATURAL_BLOCKS = COL_QKV + 3
LANE_GROUPS = GROUP_W // LANES
MOE_BLOCK = 512
PROJ_TILE = 512
ATT_UNITS = 2

_SLOPES = [float(np.float32(2.0 ** (-8.0 * (i + 1) / N_ATT_HEADS))) for i in range(N_ATT_HEADS)]


def _params(*sem):
    return pltpu.CompilerParams(dimension_semantics=sem, vmem_limit_bytes=VMEM_LIMIT_BYTES)


def _sigmoid(x):
    return 0.5 * jnp.tanh(0.5 * x) + 0.5


def _log2(n):
    assert n > 0 and n & (n - 1) == 0, n
    return n.bit_length() - 1


def _rmsnorm_bf16(x, gain):
    ms = jnp.mean(x * x, axis=-1, keepdims=True)
    return (x * lax.rsqrt(ms + RMS_EPS) * gain).astype(BF16)


def _in_proj_kernel(x_ref, g_ref, w_ref, o_ref, xn_ref):
    @pl.when(pl.program_id(1) == 0)
    def _():
        xn_ref[...] = _rmsnorm_bf16(x_ref[...], g_ref[...])

    o_ref[...] = jnp.dot(xn_ref[...], w_ref[...], preferred_element_type=F32).astype(o_ref.dtype)


def _in_proj(x2d, gain, w, out_dtype, n_col_tiles):
    n, d = x2d.shape
    cols = w.shape[1]
    tm = min(PROJ_TILE, n)
    tn = cols // n_col_tiles
    return pl.pallas_call(
        _in_proj_kernel,
        grid=(n // tm, n_col_tiles),
        in_specs=[
            pl.BlockSpec((tm, d), lambda i, j: (i, 0)),
            pl.BlockSpec((1, d), lambda i, j: (0, 0)),
            pl.BlockSpec((d, tn), lambda i, j: (0, j)),
        ],
        out_specs=pl.BlockSpec((tm, tn), lambda i, j: (i, j)),
        out_shape=jax.ShapeDtypeStruct((n, cols), out_dtype),
        scratch_shapes=[pltpu.VMEM((tm, d), BF16)],
        compiler_params=_params("parallel", "arbitrary"),
        name="in_proj",
    )(x2d, gain, w)


def _in_proj_prompt_kernel(x_ref, g_ref, w_ref, *refs, dilations, n_chunks):
    n_dil = len(dilations)
    perm_refs, zn_ref, zd_refs = refs[:n_dil], refs[n_dil], refs[n_dil + 1:]
    xn = _rmsnorm_bf16(x_ref[...], g_ref[...])
    nat = zn_ref.shape[1]
    cw = nat // n_chunks
    for c in range(n_chunks):
        cs = slice(c * cw, (c + 1) * cw)
        zn_ref[:, cs] = jnp.dot(xn, w_ref[:, cs], preferred_element_type=F32).astype(zn_ref.dtype)
    col = nat
    for perm_ref, zd_ref, dil in zip(perm_refs, zd_refs, dilations):
        width = zd_ref.shape[-1]
        xp = jnp.dot(perm_ref[...], xn, preferred_element_type=F32).astype(BF16)
        res = jnp.dot(xp, w_ref[:, col:col + width], preferred_element_type=F32).astype(zd_ref.dtype)
        per = res.shape[0] // dil
        for r in range(dil):
            zd_ref[r] = res[r * per:(r + 1) * per, :]
        col += width


def _in_proj_prompt(x2d, gain, w, seq, nat, dilations):
    n, d = x2d.shape
    cols = w.shape[1]
    tm = PROJ_TILE // 2
    width = 3 * GROUP_W
    assert nat + width * len(dilations) == cols and seq % tm == 0
    tiles_per_seq = seq // tm
    perms = []
    for dil in dilations:
        per = tm // dil
        o = np.arange(tm)
        perm = np.zeros((tm, tm), np.float32)
        perm[o, (o % per) * dil + o // per] = 1.0
        perms.append(jnp.asarray(perm, BF16))
    const = lambda shape: pl.BlockSpec(shape, lambda i: (0,) * len(shape))
    outs = pl.pallas_call(
        functools.partial(_in_proj_prompt_kernel, dilations=tuple(dilations), n_chunks=2),
        grid=(n // tm,),
        in_specs=[pl.BlockSpec((tm, d), lambda i: (i, 0)), const((1, d)),
                  pl.BlockSpec((d, cols), lambda i: (0, 0), pipeline_mode=pl.Buffered(1))]
        + [const((tm, tm))] * len(dilations),
        out_specs=[pl.BlockSpec((tm, nat), lambda i: (i, 0))]
        + [pl.BlockSpec((None, dil, tm // dil, width), lambda i: (i // tiles_per_seq, 0, i % tiles_per_seq, 0))
           for dil in dilations],
        out_shape=[jax.ShapeDtypeStruct((n, nat), BF16)]
        + [jax.ShapeDtypeStruct((n // seq, dil, seq // dil, width), BF16) for dil in dilations],
        compiler_params=_params("parallel"),
        name="in_proj_prompt",
    )(x2d, gain, w, *perms)
    return outs[0], outs[1:]


def _attn_prompt_kernel(q_ref, kp_ref, kc_ref, vp_ref, vc_ref, b_ref, o_ref, l_ref, *, dilation, n_sub, units):
    blk = ATT_BLOCK
    j = pl.program_id(1)
    low = lax.broadcasted_iota(jnp.int32, (blk, LANES), 1) < HEAD_DIM

    def unit(r, u):
        own = slice(u * blk, (u + 1) * blk)
        q = q_ref[r, own, :] * (HEAD_DIM ** -0.5)
        if u == 0:
            k = jnp.concatenate([kp_ref[r], kc_ref[r, own, :]], axis=0)
            v = jnp.concatenate([vp_ref[r], vc_ref[r, own, :]], axis=0)
            bias = b_ref.at[jnp.minimum(j, 1)]
        else:
            both = slice((u - 1) * blk, (u + 1) * blk)
            k, v = kc_ref[r, both, :], vc_ref[r, both, :]
            bias = b_ref.at[1]
        rows = pl.ds(u * blk * dilation + r, blk, stride=dilation) if dilation > 1 else own
        for p in range(LANE_GROUPS):
            cs = slice(p * LANES, (p + 1) * LANES)
            qg, kg, vg = q[:, cs], k[:, cs], v[:, cs]
            outs, lses = [], []
            for hh in range(2):
                sel = low if hh == 0 else jnp.logical_not(low)
                qm = jnp.where(sel, qg, jnp.zeros_like(qg))
                s = lax.dot_general(qm, kg, (((1,), (1,)), ((), ())), preferred_element_type=F32)
                s = s + bias[2 * p + hh]
                m = jnp.max(s, axis=-1, keepdims=True)
                e = jnp.exp(s - m)
                den = jnp.sum(e, axis=-1, keepdims=True)
                outs.append(jnp.dot(e.astype(BF16), vg, preferred_element_type=F32) / den)
                lses.append(m + jnp.log(den))
            o_ref[p, rows, :] = jnp.where(low, outs[0], outs[1])
            l_ref[p, rows, :] = jnp.where(low, lses[0], lses[1])

    per_r = min(units, n_sub)
    r_per_body = units // per_r
    assert n_sub % per_r == 0 and dilation % r_per_body == 0

    def body(i, carry):
        for rr in range(r_per_body):
            for u in range(n_sub):
                unit(i * r_per_body + rr, u)
        return carry

    if dilation == r_per_body:
        body(0, 0)
    else:
        lax.fori_loop(0, dilation // r_per_body, body, 0)


def _band_bias(g):
    blk, d = ATT_BLOCK, ATT_DILATIONS[g]
    qq = np.arange(blk)[:, None]
    kk = np.arange(2 * blk)[None, :]
    step = qq + blk - kk
    band = (step >= 0) & (step <= blk)
    slopes = np.asarray(_SLOPES[g * HEADS_PER_GROUP:(g + 1) * HEADS_PER_GROUP], np.float32)
    bias = -slopes[:, None, None] * (step * d).astype(np.float32)[None]
    out = np.empty((2, HEADS_PER_GROUP, blk, 2 * blk), np.float32)
    out[0] = np.where(band & (kk >= blk), bias, -np.inf)
    out[1] = np.where(band, bias, -np.inf)
    return jnp.asarray(out)


def _attn_prompt(zq, g, col0):
    d = ATT_DILATIONS[g]
    batch, _, n, _ = zq.shape
    n_sub = max(1, ATT_UNITS // d)
    nb = n // (ATT_BLOCK * n_sub)
    span = ATT_BLOCK * n_sub

    def cur(c):
        return pl.BlockSpec((None, d, span, GROUP_W), lambda b, j: (b, 0, j, c))

    def prev(c):
        return pl.BlockSpec((None, d, ATT_BLOCK, GROUP_W), lambda b, j: (b, 0, jnp.maximum(j * n_sub - 1, 0), c))

    out_spec = pl.BlockSpec((LANE_GROUPS, span * d, LANES), lambda b, j: (0, b * nb + j, 0))
    out_sds = jax.ShapeDtypeStruct((LANE_GROUPS, batch * n * d, LANES), F32)
    bias = _band_bias(g)
    return pl.pallas_call(
        functools.partial(_attn_prompt_kernel, dilation=d, n_sub=n_sub, units=ATT_UNITS),
        grid=(batch, nb),
        in_specs=[cur(col0), prev(col0 + 1), cur(col0 + 1), prev(col0 + 2), cur(col0 + 2),
                  pl.BlockSpec(bias.shape, lambda b, j: (0, 0, 0, 0))],
        out_specs=[out_spec, out_spec],
        out_shape=[out_sds, out_sds],
        compiler_params=_params("parallel", "arbitrary"),
        name=f"attn_prompt_g{g}",
    )(zq, zq, zq, zq, zq, bias)


def _attn_sample_kernel(q_ref, k_ref, v_ref, c_ref, sl_ref, o_ref, l_ref, cout_ref, *, window, dilation):
    t_new = q_ref.shape[0]
    rows = HEADS_PER_GROUP * t_new
    q = q_ref[...] * (HEAD_DIM ** -0.5)
    pad = jnp.zeros((LANES - t_new, GROUP_W), F32)
    kn = jnp.concatenate([k_ref[...], pad], axis=0)
    vn = jnp.concatenate([v_ref[...], pad], axis=0)
    new_t = jnp.concatenate([kn.T, vn.T], axis=0)
    cin = c_ref[...]
    cout_ref[...] = pltpu.roll(cin, window - t_new, axis=1)
    tail = cout_ref[:, window - LANES:window]
    lane = lax.broadcasted_iota(jnp.int32, tail.shape, 1)
    cout_ref[:, window - LANES:window] = jnp.where(lane >= LANES - t_new,
                                                   pltpu.roll(new_t, LANES - t_new, axis=1), tail)
    qt = jnp.concatenate([q] * HEADS_PER_GROUP, axis=0)
    row_h = lax.broadcasted_iota(jnp.int32, (rows, GROUP_W), 0) >> _log2(t_new)
    col_h = lax.broadcasted_iota(jnp.int32, (rows, GROUP_W), 1) >> _log2(HEAD_DIM)
    qbd = jnp.where(row_h == col_h, qt, 0.0).astype(BF16)
    k_t = cin[0:GROUP_W, :].astype(BF16)
    v_t = cin[GROUP_W:2 * GROUP_W, :].astype(BF16)
    s_c = jnp.dot(qbd, k_t, preferred_element_type=F32)
    s_n = jnp.dot(qbd, new_t[0:GROUP_W, :].astype(BF16), preferred_element_type=F32)
    _log2(dilation)
    slope = sl_ref[:, 0:1]
    t_c = lax.broadcasted_iota(jnp.int32, s_c.shape, 0) & (t_new - 1)
    dist_c = window + t_c - lax.broadcasted_iota(jnp.int32, s_c.shape, 1)
    s_c = jnp.where((dist_c <= window) & ((dist_c & (dilation - 1)) == 0), s_c - slope * dist_c.astype(F32), NEG_INF)
    t_n = lax.broadcasted_iota(jnp.int32, s_n.shape, 0) & (t_new - 1)
    dist_n = t_n - lax.broadcasted_iota(jnp.int32, s_n.shape, 1)
    s_n = jnp.where((dist_n >= 0) & ((dist_n & (dilation - 1)) == 0), s_n - slope * dist_n.astype(F32), NEG_INF)
    m = jnp.maximum(jnp.max(s_c, axis=-1, keepdims=True), jnp.max(s_n, axis=-1, keepdims=True))
    e_c = jnp.exp(s_c - m)
    e_n = jnp.exp(s_n - m)
    den = jnp.sum(e_c, axis=-1, keepdims=True) + jnp.sum(e_n, axis=-1, keepdims=True)
    o_full = lax.dot_general(e_c.astype(BF16), v_t, (((1,), (1,)), ((), ())), preferred_element_type=F32)
    o_full = (o_full + jnp.dot(e_n.astype(BF16), vn.astype(BF16), preferred_element_type=F32)) / den
    lse = m + jnp.log(den)
    out_h = lax.broadcasted_iota(jnp.int32, (t_new, GROUP_W), 1) >> _log2(HEAD_DIM)
    o = jnp.zeros((t_new, GROUP_W), F32)
    l = jnp.zeros((t_new, GROUP_W), F32)
    for h in range(HEADS_PER_GROUP):
        rs = slice(h * t_new, (h + 1) * t_new)
        o = jnp.where(out_h == h, o_full[rs, :], o)
        l = jnp.where(out_h == h, lse[rs, :], l)
    for c in range(LANE_GROUPS):
        o_ref[c] = o[:, c * LANES:(c + 1) * LANES]
        l_ref[c] = l[:, c * LANES:(c + 1) * LANES]


def _attn_sample(z, cache, g):
    batch, window = cache.shape[0], cache.shape[1]
    n_tok = z.shape[0]
    t_new = n_tok // batch
    d = ATT_DILATIONS[g]
    c_t = jnp.transpose(cache, (0, 2, 3, 4, 1)).reshape(batch, 2 * GROUP_W, window)
    slopes = np.repeat(np.asarray(_SLOPES[g * HEADS_PER_GROUP:(g + 1) * HEADS_PER_GROUP], np.float32), t_new)
    slopes = jnp.asarray(np.broadcast_to(slopes[:, None], (HEADS_PER_GROUP * t_new, LANES)))
    col0 = COL_QKV + 3 * g

    def col(c):
        return pl.BlockSpec((t_new, GROUP_W), lambda b: (b, c))

    tok_spec = pl.BlockSpec((LANE_GROUPS, t_new, LANES), lambda b: (0, b, 0))
    tok_sds = jax.ShapeDtypeStruct((LANE_GROUPS, n_tok, LANES), F32)
    buf_spec = pl.BlockSpec((None, 2 * GROUP_W, window), lambda b: (b, 0, 0))
    o, l, cout = pl.pallas_call(
        functools.partial(_attn_sample_kernel, window=window, dilation=d),
        grid=(batch,),
        in_specs=[col(col0), col(col0 + 1), col(col0 + 2), buf_spec, pl.BlockSpec(slopes.shape, lambda b: (0, 0))],
        out_specs=[tok_spec, tok_spec, buf_spec],
        out_shape=[tok_sds, tok_sds, jax.ShapeDtypeStruct(c_t.shape, F32)],
        compiler_params=_params("parallel"),
        name=f"attn_sample_g{g}",
    )(z, z, z, c_t, slopes)
    cout = cout.reshape(batch, 2, HEADS_PER_GROUP, HEAD_DIM, window)
    return o, l, jnp.transpose(cout, (0, 4, 1, 2, 3))


def _gelu_tanh(x):
    c1 = float(np.sqrt(2.0 / np.pi))
    return x * (0.5 * jnp.tanh(x * (c1 + (c1 * 0.044715) * (x * x))) + 0.5)


def _rglru_kernel(xb_ref, yg_ref, c0_ref, h0_ref, cw_ref, cb_ref, wa_ref, wi_ref, ba_ref, bi_ref, lam_ref,
                  yb_ref, cout_ref, hout_ref, xpad, a_s, u_s, h_s, *, tc, seg, reset_first):
    nb = xb_ref.shape[0]
    ngl = a_s.shape[0]
    j = pl.program_id(1)

    @pl.when(j == 0)
    def _():
        xpad[:, 0:SUBLANES, :] = c0_ref[...]
        h_s[...] = h0_ref[...]

    @pl.when(j > 0)
    def _():
        xpad[:, 0:SUBLANES, :] = xpad[:, tc:tc + SUBLANES, :]

    xpad[:, SUBLANES:SUBLANES + tc, :] = xb_ref[...].astype(F32)
    cout_ref[...] = xpad[:, tc:tc + SUBLANES, :]

    lam = lam_ref[...]
    softplus_neg = jnp.maximum(-lam, 0.0) + jnp.log1p(jnp.exp(-jnp.abs(lam)))
    first = (lax.broadcasted_iota(jnp.int32, (tc, 1), 0) == 0) & (j == 0)
    gw = wa_ref.shape[1]
    for b in range(nb):
        xc = cb_ref[...]
        for tap in range(CONV_WIDTH):
            off = SUBLANES - (CONV_WIDTH - 1) + tap
            xc = xc + xpad[b, off:off + tc, :] * cw_ref[tap:tap + 1, :]
        xcb = xc.astype(BF16)
        ra, ri = [], []
        for blk in range(wa_ref.shape[0]):
            xs = xcb[:, blk * gw:(blk + 1) * gw]
            ra.append(jnp.dot(xs, wa_ref[blk], preferred_element_type=F32))
            ri.append(jnp.dot(xs, wi_ref[blk], preferred_element_type=F32))
        r = _sigmoid(jnp.concatenate(ra, axis=-1) + ba_ref[...])
        i = _sigmoid(jnp.concatenate(ri, axis=-1) + bi_ref[...])
        log_a = -LRU_C * r * softplus_neg
        a = jnp.exp(log_a)
        w = 1.0 - a * a
        mult = jnp.where(w > 0.0, w * lax.rsqrt(w), 0.0)
        if reset_first:
            mult = jnp.where(first, 1.0, mult)
        u = mult * (i * xc)
        for c in range(ngl):
            a_s[c, b * seg:b * seg + tc, :] = a[:, c * LANES:(c + 1) * LANES]
            u_s[c, b * seg:b * seg + tc, :] = u[:, c * LANES:(c + 1) * LANES]

    def step(t, hs):
        idx = pl.ds(t, nb, stride=seg)
        out = []
        for c in range(ngl):
            h = a_s[c, idx, :] * hs[c] + u_s[c, idx, :]
            u_s[c, idx, :] = h
            out.append(h)
        return tuple(out)

    hs = lax.fori_loop(0, tc, step, tuple(h_s[:, c * LANES:(c + 1) * LANES] for c in range(ngl)))
    h = jnp.concatenate(hs, axis=-1)
    h_s[...] = h
    hout_ref[...] = h
    for b in range(nb):
        hb = jnp.concatenate([u_s[c, b * seg:b * seg + tc, :] for c in range(ngl)], axis=-1)
        yb_ref[b] = (_gelu_tanh(yg_ref[b].astype(F32)) * hb).astype(yb_ref.dtype)


def _rglru(z3, conv0, h0, p, reset_first):
    batch, t, _ = z3.shape
    width = h0.shape[-1]
    nb = SUBLANES
    tc = min(128, t)
    seg = tc + SUBLANES
    c0 = jnp.pad(conv0, ((0, 0), (SUBLANES - (CONV_WIDTH - 1), 0), (0, 0)))
    cw = jnp.pad(p["conv_w"], ((0, SUBLANES - CONV_WIDTH), (0, 0)))
    ngrp = p["wa"].shape[0]
    gw = p["wa"].shape[1]

    def full(shape):
        return pl.BlockSpec(shape, lambda i, j: (0,) * len(shape))

    yb, cout, hout = pl.pallas_call(
        functools.partial(_rglru_kernel, tc=tc, seg=seg, reset_first=reset_first),
        grid=(batch // nb, t // tc),
        in_specs=[
            pl.BlockSpec((nb, tc, width), lambda i, j: (i, j, 0)),
            pl.BlockSpec((nb, tc, width), lambda i, j: (i, j, 1)),
            pl.BlockSpec((nb, SUBLANES, width), lambda i, j: (i, 0, 0)),
            pl.BlockSpec((nb, width), lambda i, j: (i, 0)),
            full((SUBLANES, width)), full((1, width)),
            full((ngrp, gw, gw)), full((ngrp, gw, gw)),
            full((1, width)), full((1, width)), full((1, width)),
        ],
        out_specs=[
            pl.BlockSpec((nb, tc, width), lambda i, j: (i, j, 0)),
            pl.BlockSpec((nb, SUBLANES, width), lambda i, j: (i, 0, 0)),
            pl.BlockSpec((nb, width), lambda i, j: (i, 0)),
        ],
        out_shape=[
            jax.ShapeDtypeStruct((batch, t, width), BF16),
            jax.ShapeDtypeStruct((batch, SUBLANES, width), F32),
            jax.ShapeDtypeStruct((batch, width), F32),
        ],
        scratch_shapes=[
            pltpu.VMEM((nb, tc + SUBLANES, width), F32),
            pltpu.VMEM((width // LANES, nb * seg, LANES), F32),
            pltpu.VMEM((width // LANES, nb * seg, LANES), F32),
            pltpu.VMEM((nb, width), F32),
        ],
        compiler_params=_params("parallel", "arbitrary"),
        name="rglru",
    )(z3, z3, c0, h0, cw, p["conv_b"], p["wa"], p["wi"], p["ba"], p["bi"], p["lam"])
    return yb, cout[:, SUBLANES - (CONV_WIDTH - 1):, :], hout


def _mix_kernel(o1, o2, o3, l1, l2, l3, yb_ref, ga_ref, gb_ref, x_ref, wpa_ref, wpb_ref, wo_ref, nf_ref,
                wr_ref, br_ref, *rest):
    x1_ref, xn_ref, idx_ref, gate_ref = rest[-4:]
    parts = []
    for c in range(LANE_GROUPS):
        la, lb, lc = l1[c], l2[c], l3[c]
        m = jnp.maximum(jnp.maximum(la, lb), lc)
        ea, eb, ec = jnp.exp(la - m), jnp.exp(lb - m), jnp.exp(lc - m)
        inv = 1.0 / (ea + eb + ec)
        parts.append((ea * inv) * o1[c] + (eb * inv) * o2[c] + (ec * inv) * o3[c])
    att = jnp.concatenate(parts, axis=-1)
    y_a = jnp.dot(att.astype(BF16), wpa_ref[...], preferred_element_type=F32)
    y_b = jnp.dot(yb_ref[...], wpb_ref[...], preferred_element_type=F32)
    merged = _sigmoid(ga_ref[...].astype(F32)) * y_a + _sigmoid(gb_ref[...].astype(F32)) * y_b
    x1 = x_ref[...] + jnp.dot(merged.astype(BF16), wo_ref[...], preferred_element_type=F32)
    x1_ref[...] = x1
    ms = jnp.mean(x1 * x1, axis=-1, keepdims=True)
    xn = (x1 * lax.rsqrt(ms + RMS_EPS) * nf_ref[...]).astype(BF16)
    xn_ref[...] = xn
    logits = jnp.dot(xn, wr_ref[...], preferred_element_type=F32) + br_ref[...]
    lane = lax.broadcasted_iota(jnp.int32, logits.shape, 1)
    lane_f = lane.astype(F32)
    logits = jnp.where(lane < N_EXPERTS, logits, NEG_INF)
    vals, idxs = [], []
    for _ in range(TOP_K):
        mk = jnp.max(logits, axis=-1, keepdims=True)
        ik = jnp.min(jnp.where(logits == mk, lane_f, float(LANES)), axis=-1, keepdims=True)
        logits = jnp.where(lane_f == ik, NEG_INF, logits)
        vals.append(mk)
        idxs.append(ik)
    es = [jnp.exp(vk - vals[0]) for vk in vals]
    tot = es[0] + es[1] + es[2] + es[3]
    idx_out = jnp.zeros(logits.shape, F32)
    gate_out = jnp.zeros(logits.shape, F32)
    for kk in range(TOP_K):
        idx_out = jnp.where(lane == kk, idxs[kk], idx_out)
        gate_out = jnp.where(lane == kk, es[kk] / tot, gate_out)
    idx_ref[...] = jnp.transpose(idx_out)[0:SUBLANES, :].astype(jnp.int32)
    gate_ref[...] = gate_out


def _mix(os_, ls_, yb, z, x2d, p, n_all, row0, xn_buf=None):
    n, d = x2d.shape
    tm = min(512, n)
    assert row0 % tm == 0

    def rows(w, c=0):
        return pl.BlockSpec((tm, w), lambda i: (i, c))

    def full(a):
        return pl.BlockSpec(a.shape, lambda i: (0,) * a.ndim)

    grp = pl.BlockSpec((LANE_GROUPS, tm, LANES), lambda i: (0, i, 0))
    in_specs = [grp] * 6 + [rows(d), rows(d, 2), rows(d, 3), rows(d),
                            full(p["w_pa"]), full(p["w_pb"]), full(p["w_o"]), full(p["norm_ffn"]),
                            full(p["w_router"]), full(p["b_router"])]
    args = [*os_, *ls_, yb, z, z, x2d, p["w_pa"], p["w_pb"], p["w_o"], p["norm_ffn"], p["w_router"], p["b_router"]]
    aliases = {}
    if xn_buf is not None:
        aliases = {len(args): 1}
        in_specs.append(pl.BlockSpec(memory_space=pl.ANY))
        args.append(xn_buf)
    x1, xn, idx, gate = pl.pallas_call(
        _mix_kernel,
        grid=(n // tm,),
        in_specs=in_specs,
        out_specs=[rows(d), pl.BlockSpec((tm, d), lambda i: (i + row0 // tm, 0)),
                   pl.BlockSpec((SUBLANES, tm), lambda i: (0, i)), rows(LANES)],
        out_shape=[jax.ShapeDtypeStruct((n, d), F32), jax.ShapeDtypeStruct((n_all, d), BF16),
                   jax.ShapeDtypeStruct((SUBLANES, n), jnp.int32), jax.ShapeDtypeStruct((n, LANES), F32)],
        input_output_aliases=aliases,
        compiler_params=_params("parallel"),
        name="mix",
    )(*args)
    return x1, xn, idx, gate


def _expert_kernel(be_ref, nused_ref, x_ref, w1_ref, b1_ref, w2_ref, b2_ref, perm_ref, y_ref, w1_s, w2_s):
    i = pl.program_id(0)
    dff = w2_ref.shape[0]
    n_cb = w1_ref.shape[1] // MXU_DIM
    half = MXU_DIM // 2

    @pl.when((i == 0) | (be_ref[i] != be_ref[jnp.maximum(i - 1, 0)]))
    def _():
        for cb in range(n_cb):
            cs = slice(cb * MXU_DIM, (cb + 1) * MXU_DIM)
            w1_s[:, cs] = jnp.dot(w1_ref[:, cs].astype(BF16), perm_ref[...], preferred_element_type=F32).astype(BF16)
        w2_s[...] = w2_ref[...].astype(BF16)

    @pl.when(i < nused_ref[0])
    def _():
        h = jnp.dot(x_ref[...], w1_s[...], preferred_element_type=F32) + b1_ref[...]
        acts = []
        for cb in range(n_cb):
            x_glu = jnp.minimum(h[:, cb * MXU_DIM:cb * MXU_DIM + half], SWIGLU_LIMIT)
            x_lin = jnp.clip(h[:, cb * MXU_DIM + half:(cb + 1) * MXU_DIM], -SWIGLU_LIMIT, SWIGLU_LIMIT)
            acts.append(x_glu * _sigmoid(SWIGLU_ALPHA * x_glu) * (x_lin + 1.0))
        act = jnp.concatenate(acts, axis=-1)
        assert act.shape[-1] == dff
        y = jnp.dot(act.astype(BF16), w2_s[...], preferred_element_type=F32) + b2_ref[...]
        y_ref[...] = y.astype(y_ref.dtype)

    @pl.when(i >= nused_ref[0])
    def _():
        y_ref[...] = jnp.zeros_like(y_ref)


def _experts(x_rows, block_expert, n_used, p, block):
    n_rows, d = x_rows.shape
    n_blocks = n_rows // block
    dff2 = p["w1"].shape[-1]
    half = MXU_DIM // 2
    o = np.arange(MXU_DIM)
    perm = np.zeros((MXU_DIM, MXU_DIM), np.float32)
    perm[np.where(o < half, 2 * o, 2 * (o - half) + 1), o] = 1.0
    grid_spec = pltpu.PrefetchScalarGridSpec(
        num_scalar_prefetch=2,
        grid=(n_blocks,),
        in_specs=[
            pl.BlockSpec((block, d), lambda i, be, nu: (i, 0)),
            pl.BlockSpec((None, d, dff2), lambda i, be, nu: (be[i], 0, 0)),
            pl.BlockSpec((None, 1, dff2), lambda i, be, nu: (be[i], 0, 0)),
            pl.BlockSpec((None, dff2 // 2, d), lambda i, be, nu: (be[i], 0, 0)),
            pl.BlockSpec((None, 1, d), lambda i, be, nu: (be[i], 0, 0)),
            pl.BlockSpec((MXU_DIM, MXU_DIM), lambda i, be, nu: (0, 0)),
        ],
        out_specs=pl.BlockSpec((block, d), lambda i, be, nu: (i, 0)),
        scratch_shapes=[pltpu.VMEM((d, dff2), BF16), pltpu.VMEM((dff2 // 2, d), BF16)],
    )
    return pl.pallas_call(
        _expert_kernel,
        grid_spec=grid_spec,
        out_shape=jax.ShapeDtypeStruct((n_rows, d), BF16),
        compiler_params=_params("arbitrary"),
        name="experts",
    )(block_expert, n_used, x_rows, p["w1"], p["b1"], p["w2"], p["b2"], jnp.asarray(perm, BF16))


def _combine_kernel(y0, y1, y2, y3, gate_ref, x1_ref, nf_ref, *rest):
    o_ref = rest[-1]
    g = gate_ref[...]
    y = jnp.zeros(x1_ref.shape, F32)
    for kk, y_ref in enumerate((y0, y1, y2, y3)):
        y = y + y_ref[...].astype(F32) * g[:, kk:kk + 1]
    x = x1_ref[...] + y
    ms = jnp.mean(x * x, axis=-1, keepdims=True)
    o_ref[...] = x * lax.rsqrt(ms + RMS_EPS) * nf_ref[...]


def _combine(ys, gate, x1, norm_final, row0=0, out_buf=None):
    n, d = ys[0].shape
    tm = min(256, n)
    assert row0 % tm == 0
    off = row0 // tm
    part = pl.BlockSpec((tm, d), lambda i: (i, 0))
    row = pl.BlockSpec((tm, d), lambda i: (i + off, 0))
    in_specs = [part] * TOP_K + [pl.BlockSpec((tm, LANES), lambda i: (i + off, 0)), row,
                                 pl.BlockSpec((1, d), lambda i: (0, 0))]
    args = [*ys, gate, x1, norm_final]
    aliases = {}
    if out_buf is not None:
        aliases = {len(args): 0}
        in_specs.append(pl.BlockSpec(memory_space=pl.ANY))
        args.append(out_buf)
    return pl.pallas_call(
        _combine_kernel,
        grid=(n // tm,),
        in_specs=in_specs,
        out_specs=row,
        out_shape=jax.ShapeDtypeStruct(x1.shape, F32),
        input_output_aliases=aliases,
        compiler_params=_params("parallel"),
        name="combine",
    )(*args)


def _rank_kernel(idx_ref, tri_ref, rank_ref, cnt_ref, carry):
    @pl.when(pl.program_id(0) == 0)
    def _():
        carry[...] = jnp.zeros_like(carry)

    idx = idx_ref[...]
    tm = idx.shape[1]
    expert = lax.broadcasted_iota(jnp.int32, (N_EXPERTS, tm), 0)
    hits = [expert == idx[kk:kk + 1, :] for kk in range(TOP_K)]
    onehot = jnp.zeros((N_EXPERTS, tm), F32)
    for hit in hits:
        onehot = onehot + jnp.where(hit, 1.0, 0.0)
    before = jnp.dot(onehot.astype(BF16), tri_ref[...], preferred_element_type=F32) + carry[:, 0:1]
    row = lax.broadcasted_iota(jnp.int32, idx.shape, 0)
    out = jnp.zeros(idx.shape, F32)
    for kk, hit in enumerate(hits):
        out = jnp.where(row == kk, jnp.sum(jnp.where(hit, before, 0.0), axis=0, keepdims=True), out)
    rank_ref[...] = out.astype(jnp.int32)
    carry[...] = carry[...] + jnp.sum(onehot, axis=1, keepdims=True)
    cnt_ref[...] = carry[...].astype(jnp.int32)


def _rank(idx_t):
    n = idx_t.shape[1]
    tm = 512
    while n % tm:
        tm //= 2
    tri = jnp.asarray(np.triu(np.ones((tm, tm), np.float32), 1), BF16)
    return pl.pallas_call(
        _rank_kernel,
        grid=(n // tm,),
        in_specs=[pl.BlockSpec((SUBLANES, tm), lambda i: (0, i)), pl.BlockSpec((tm, tm), lambda i: (0, 0))],
        out_specs=[pl.BlockSpec((SUBLANES, tm), lambda i: (0, i)), pl.BlockSpec((N_EXPERTS, LANES), lambda i: (0, 0))],
        out_shape=[jax.ShapeDtypeStruct((SUBLANES, n), jnp.int32), jax.ShapeDtypeStruct((N_EXPERTS, LANES), jnp.int32)],
        scratch_shapes=[pltpu.VMEM((N_EXPERTS, LANES), F32)],
        compiler_params=_params("arbitrary"),
        name="rank",
    )(idx_t, tri)


def _route(idx_t, block):
    n_tok = idx_t.shape[1]
    n_assign = n_tok * TOP_K
    tok_bits = max(n_tok - 1, 1).bit_length()
    assert N_EXPERTS << tok_bits < 2 ** 31
    rank_t, cnt = _rank(idx_t)
    counts = cnt[:, 0]
    e_t = idx_t[:TOP_K]
    keys = jnp.sort(((e_t << tok_bits) + jnp.arange(n_tok, dtype=jnp.int32)[None, :]).reshape(-1))
    padded = (counts + block - 1) // block * block
    starts = jnp.cumsum(counts) - counts
    pends = jnp.cumsum(padded)
    pstarts = pends - padded
    experts = jnp.arange(N_EXPERTS, dtype=jnp.int32)[:, None, None]
    dest_t = jnp.sum(jnp.where(e_t[None] == experts, pstarts[:, None, None], 0), axis=0) + rank_t[:TOP_K]
    n_blocks = -(-(n_assign + N_EXPERTS * (block - 1)) // block)
    n_rows = n_blocks * block
    block_start = jnp.arange(n_blocks, dtype=jnp.int32) * block
    block_expert = jnp.minimum(jnp.sum((pends[None, :] <= block_start[:, None]).astype(jnp.int32), axis=1),
                               N_EXPERTS - 1)
    local = block_start - pstarts[block_expert]
    first = jnp.repeat(starts[block_expert] + local, block)
    live = jnp.repeat(counts[block_expert] - local, block)
    within = jnp.tile(jnp.arange(block, dtype=jnp.int32), n_blocks)
    src = keys[jnp.clip(first + within, 0, n_assign - 1)] & ((1 << tok_bits) - 1)
    row = jnp.arange(n_rows, dtype=jnp.int32)
    row_tok = jnp.where(within < live, src, row % n_tok)
    n_used = (pends[-1] // block).astype(jnp.int32).reshape(1)
    return row_tok, dest_t, block_expert, n_used


def _mixer(x, p, caches, conv0, h0, n_all, row0, xn_buf):
    batch, t, d = x.shape
    prompt = caches is None
    x2d = x.reshape(batch * t, d)
    os_, ls_, new_kv = [], [], []
    if prompt:
        nat = N_NATURAL_BLOCKS * GROUP_W
        z, zd = _in_proj_prompt(x2d, p["norm_mix"], p["w_in"], t, nat, ATT_DILATIONS[1:])
        zq = [z.reshape(batch, 1, t, nat)] + list(zd)
        for g in range(N_GROUPS):
            dil = ATT_DILATIONS[g]
            o, l = _attn_prompt(zq[g], g, COL_QKV if g == 0 else 0)
            keep = min(ATT_WINDOWS[g], t)
            if g == 0:
                kcol = (COL_QKV + 1) * GROUP_W
                last = z.reshape(batch, t, nat)[:, t - keep:, kcol:kcol + 2 * GROUP_W]
            else:
                last = zq[g][:, :, (t - keep) // dil:, GROUP_W:3 * GROUP_W]
                last = jnp.swapaxes(last, 1, 2).reshape(batch, keep, 2 * GROUP_W)
            kv = last.astype(F32).reshape(batch, keep, 2, HEADS_PER_GROUP, HEAD_DIM)
            os_.append(o)
            ls_.append(l)
            new_kv.append(kv[None])
        conv0 = jnp.zeros((batch, CONV_WIDTH - 1, d), F32)
        h0 = jnp.zeros((batch, d), F32)
    else:
        z = _in_proj(x2d, p["norm_mix"], p["w_in"], F32, 4)
        for g in range(N_GROUPS):
            o, l, kv = _attn_sample(z, caches[g], g)
            os_.append(o)
            ls_.append(l)
            new_kv.append(kv[None])
    yb, new_conv, h_last = _rglru(z.reshape(batch, t, -1), conv0, h0, p, prompt)
    x1, xn, idx, gate = _mix(os_, ls_, yb.reshape(batch * t, d), z, x2d, p, n_all, row0, xn_buf)
    return x1, xn, idx, gate, new_kv, new_conv[None], h_last[None]


def kernel(x_prompt, x_sample, cache_kv_g1, cache_kv_g2, cache_kv_g3, state_conv, state_h, norm_mix, w_in, w_pa,
           w_pb, w_o, conv_w, conv_b, lru_wa, lru_ba, lru_wi, lru_bi, lru_lambda, norm_ffn, w_router, b_router,
           w1, b1, w2, b2, norm_final):
    d = x_prompt.shape[-1]
    gsz = N_GROUPS * GROUP_W

    def block_diag(w):
        per = MXU_DIM // LRU_BLOCK_W
        w = w.reshape(-1, per, LRU_BLOCK_W, LRU_BLOCK_W)
        eye = jnp.eye(per, dtype=w.dtype)
        return jnp.einsum("gacd,ab->gacbd", w, eye).reshape(-1, MXU_DIM, MXU_DIM).astype(BF16)

    w_in0 = w_in[0]
    qkv = [w_in0[:, which * gsz + g * GROUP_W: which * gsz + (g + 1) * GROUP_W]
           for g in range(N_GROUPS) for which in range(3)]
    n_e, dff2 = b1.shape[1], b1.shape[2]
    half = MXU_DIM // 2
    p = dict(
        norm_mix=norm_mix[0][None], norm_ffn=norm_ffn[0][None],
        w_in=jnp.concatenate([w_in0[:, 3 * gsz:]] + qkv, axis=1).astype(BF16),
        w_pa=w_pa[0].astype(BF16), w_pb=w_pb[0].astype(BF16), w_o=w_o[0].astype(BF16),
        conv_w=conv_w[0], conv_b=conv_b[0][None],
        wa=block_diag(lru_wa[0]), wi=block_diag(lru_wi[0]),
        ba=lru_ba[0].reshape(1, d), bi=lru_bi[0].reshape(1, d), lam=lru_lambda[0][None],
        w_router=jnp.pad(w_router[0], ((0, 0), (0, LANES - N_EXPERTS))).astype(BF16),
        b_router=jnp.pad(b_router[0], (0, LANES - N_EXPERTS))[None],
        w1=w1[0], w2=w2[0],
        b1=b1[0].reshape(n_e, dff2 // MXU_DIM, half, 2).transpose(0, 1, 3, 2).reshape(n_e, 1, dff2),
        b2=b2[0][:, None, :],
    )

    nf = norm_final[None]

    n_p = x_prompt.shape[0] * x_prompt.shape[1]
    n_all = n_p + x_sample.shape[0] * x_sample.shape[1]
    x1_p, xn_all, idx_p, gate_p, kv_p, conv_p, h_p = _mixer(x_prompt, p, None, None, None, n_all, 0, None)
    x1_s, xn_all, idx_s, gate_s, kv_s, conv_s, h_s = _mixer(
        x_sample, p, [cache_kv_g1[0], cache_kv_g2[0], cache_kv_g3[0]], state_conv[0], state_h[0], n_all, n_p, xn_all)

    half = n_p // 2
    tok_a, dest_a, be_a, nu_a = _route(idx_p[:, :half], MOE_BLOCK)
    tok_b, dest_b, be_b, nu_b = _route(jnp.concatenate([idx_p[:, half:], idx_s], axis=1), MOE_BLOCK)
    x_a = xn_all[tok_a]
    x_b = xn_all[tok_b + half]
    y_a = _experts(x_a, be_a, nu_a, p, MOE_BLOCK)
    y_b = _experts(x_b, be_b, nu_b, p, MOE_BLOCK)
    y_p = _combine([y_a[dest_a[kk]] for kk in range(TOP_K)], gate_p, x1_p, nf)
    y_p = _combine([y_b[dest_b[kk, :n_p - half]] for kk in range(TOP_K)], gate_p, x1_p, nf, row0=half, out_buf=y_p)
    y_s = _combine([y_b[dest_b[kk, n_p - half:]] for kk in range(TOP_K)], gate_s, x1_s, nf)
    y_p, y_s = y_p.reshape(x_prompt.shape), y_s.reshape(x_sample.shape)
    return (y_p, y_s, kv_p[0], kv_p[1], kv_p[2], conv_p, h_p, kv_s[0], kv_s[1], kv_s[2], conv_s, h_s)
```
